```python
import jax, jax.numpy as jnp
from jax import lax
import numpy as np

D_MODEL = 2048
BATCH = 2
SEQ = 16384
DEPTH = 2

N_EVEN = (DEPTH + 1) // 2
N_ODD = DEPTH // 2
DN_ALPHA = (2.0 * DEPTH) ** 0.25
DN_BETA = (8.0 * DEPTH) ** -0.25
LN_EPS = 1e-5

RWKV_WIDTH = D_MODEL // 2
RWKV_HEAD = 64
RWKV_HEADS = RWKV_WIDTH // RWKV_HEAD
RWKV_DECAY_LORA = 96
RWKV_AAA_LORA = 96
RWKV_GATE_LORA = 64
RWKV_GN_EPS = 64e-5
RWKV_COLS = 3 * RWKV_WIDTH + RWKV_DECAY_LORA + RWKV_AAA_LORA + RWKV_GATE_LORA
GDN_WIDTH = D_MODEL - RWKV_WIDTH
GDN_HEAD = 128
GDN_HEADS = GDN_WIDTH // GDN_HEAD
GDN_CONV = 4
GDN_CHUNK = 64
GDN_NORM_EPS = 1e-6
GDN_COLS = 4 * GDN_WIDTH + 2 * GDN_HEADS
EV_IN_COLS = RWKV_COLS + GDN_COLS

LRU_WIDTH = D_MODEL
LRU_BLOCKS = 8
LRU_BLOCK = LRU_WIDTH // LRU_BLOCKS
LRU_CONV = 4
LRU_C = 8.0

PEER_HEADS = 8
PEER_KEY_DIM = 256
PEER_HALF = PEER_KEY_DIM // 2
PEER_N_KEYS = 128
PEER_N_EXPERTS = PEER_N_KEYS * PEER_N_KEYS
PEER_TOPK = 16
PEER_TOKEN_BLOCK = 128

PLE_DIM = 256

kernel_name = 'hybrid_rwkv7_gdn_rglru_peer_deepnorm'


def layer_norm(x, g, b):
    xf = x.astype(jnp.float32)
    mu = jnp.mean(xf, -1, keepdims=True)
    var = jnp.mean(jnp.square(xf - mu), -1, keepdims=True)
    return ((xf - mu) * lax.rsqrt(var + LN_EPS) * g + b).astype(x.dtype)


def l2_normalize(x, eps=1e-6):
    xf = x.astype(jnp.float32)
    return xf * lax.rsqrt(jnp.sum(xf * xf, -1, keepdims=True) + eps)


def token_shift(x):
    return jnp.pad(x, ((0, 0), (1, 0), (0, 0)))[:, :-1]


def causal_depthwise_conv(x, w, b=None):
    k = w.shape[0]
    out = lax.conv_general_dilated(x, w[:, None, :].astype(x.dtype), window_strides=(1,),
                                   padding=[(k - 1, 0)], dimension_numbers=('NWC', 'WIO', 'NWC'),
                                   feature_group_count=x.shape[-1])
    return out if b is None else out + b


def _rwkv7_step(state, inp):
    r_t, dec_t, k_t, v_t, a_t, b_t = inp
    sa = jnp.einsum('bhvk,bhk->bhv', state, a_t)
    state = (state * dec_t[:, :, None, :] + sa[..., :, None] * b_t[..., None, :]
             + v_t[..., :, None] * k_t[..., None, :])
    return state, jnp.einsum('bhvk,bhk->bhv', state, r_t)


def rwkv7_time_mix(proj, mu, w0, w_up, a0, a_up, g_up, k_k, k_a, r_k, ln_g, ln_b):
    bsz, seq, _ = proj.shape
    W = RWKV_WIDTH
    mixed = proj + (token_shift(proj) - proj) * mu
    r, k, v, wd, ad, gd = jnp.split(
        mixed, [W, 2 * W, 3 * W, 3 * W + RWKV_DECAY_LORA, 3 * W + RWKV_DECAY_LORA + RWKV_AAA_LORA], axis=-1)
    w_log = -jax.nn.softplus(-(w0 + jnp.tanh(wd) @ w_up)) - 0.5
    a = jax.nn.sigmoid(a0 + ad @ a_up)
    g = jax.nn.sigmoid(gd) @ g_up
    heads = lambda t: t.reshape(bsz, seq, RWKV_HEADS, RWKV_HEAD).astype(jnp.float32)
    kk = l2_normalize(heads(k * k_k))
    k = heads(k * (1.0 + (a - 1.0) * k_a))
    r, v, a = heads(r), heads(v), heads(a)
    decay = jnp.exp(-jnp.exp(heads(w_log)))
    xs = tuple(jnp.moveaxis(t, 1, 0) for t in (r, decay, k, v, -kk, kk * a))
    state0 = jnp.zeros((bsz, RWKV_HEADS, RWKV_HEAD, RWKV_HEAD), jnp.float32)
    _, y = lax.scan(_rwkv7_step, state0, xs)
    y = jnp.moveaxis(y, 0, 1)
    mean = jnp.mean(y, -1, keepdims=True)
    var = jnp.mean(jnp.square(y - mean), -1, keepdims=True)
    y = ((y - mean) * lax.rsqrt(var + RWKV_GN_EPS)).reshape(bsz, seq, W) * ln_g + ln_b
    bonus = jnp.sum(r * k * r_k, -1, keepdims=True) * v
    y = y + bonus.reshape(bsz, seq, W)
    return (y * g).astype(proj.dtype)


def _gdn_chunk_step(state, inp):
    u_n, w_n, qd_n, att_n, kd_n, gl_n = inp
    v_new = u_n - jnp.einsum('bhck,bhkv->bhcv', w_n, state)
    o_n = jnp.einsum('bhck,bhkv->bhcv', qd_n, state) + jnp.einsum('bhcj,bhjv->bhcv', att_n, v_new)
    state = state * gl_n[..., None, None] + jnp.einsum('bhck,bhcv->bhkv', kd_n, v_new)
    return state, o_n


def gated_delta_rule_chunked(q, k, v, beta, g):
    bsz, seq, nh, dk = q.shape
    dv = v.shape[-1]
    C = GDN_CHUNK
    n = seq // C
    to_chunks = lambda t: jnp.moveaxis(t.reshape((bsz, n, C, nh) + t.shape[3:]), 3, 1)
    q, k, v, beta, g = (to_chunks(t) for t in (q, k, v, beta, g))
    gc = jnp.cumsum(g, axis=-1)
    causal = jnp.tril(jnp.ones((C, C), bool))
    strict = jnp.tril(jnp.ones((C, C), bool), -1)
    decay_mat = jnp.exp(jnp.where(causal, gc[..., :, None] - gc[..., None, :], -jnp.inf))
    kb = k * beta[..., None]
    m = jnp.where(strict, jnp.einsum('bhnik,bhnjk->bhnij', kb, k) * decay_mat, 0.0)
    rhs = jnp.concatenate([v * beta[..., None], kb * jnp.exp(gc)[..., None]], axis=-1)
    sol = lax.linalg.triangular_solve(m + jnp.eye(C, dtype=m.dtype), rhs, left_side=True,
                                      lower=True, unit_diagonal=True)
    u, w = sol[..., :dv], sol[..., dv:]
    attn = jnp.einsum('bhnik,bhnjk->bhnij', q, k) * decay_mat
    q_dec = q * jnp.exp(gc)[..., None]
    k_dec = k * jnp.exp(gc[..., -1:] - gc)[..., None]
    g_last = jnp.exp(gc[..., -1])
    xs = tuple(jnp.moveaxis(t, 2, 0) for t in (u, w, q_dec, attn, k_dec, g_last))
    state0 = jnp.zeros((bsz, nh, dk, dv), jnp.float32)
    _, o = lax.scan(_gdn_chunk_step, state0, xs)
    return jnp.transpose(o, (1, 0, 3, 2, 4)).reshape(bsz, seq, nh, dv)


def gated_deltanet_mix(proj, conv_w, a_log, dt_bias, norm_g):
    bsz, seq, _ = proj.shape
    W, H, Dh = GDN_WIDTH, GDN_HEADS, GDN_HEAD
    qkv, z, b_raw, a_raw = jnp.split(proj, [3 * W, 4 * W, 4 * W + H], axis=-1)
    qkv = jax.nn.silu(causal_depthwise_conv(qkv, conv_w)).reshape(bsz, seq, 3, H, Dh)
    q = l2_normalize(qkv[:, :, 0]) * Dh ** -0.5
    k = l2_normalize(qkv[:, :, 1])
    v = qkv[:, :, 2].astype(jnp.float32)
    beta = jax.nn.sigmoid(b_raw.astype(jnp.float32))
    g = -jnp.exp(a_log) * jax.nn.softplus(a_raw.astype(jnp.float32) + dt_bias)
    o = gated_delta_rule_chunked(q, k, v, beta, g)
    o = o * lax.rsqrt(jnp.mean(o * o, -1, keepdims=True) + GDN_NORM_EPS) * norm_g
    o = o * jax.nn.silu(z.reshape(bsz, seq, H, Dh).astype(jnp.float32))
    return o.reshape(bsz, seq, W).astype(proj.dtype)


def _linear_recurrence_combine(c1, c2):
    a1, b1 = c1
    a2, b2 = c2
    return a1 * a2, a2 * b1 + b2


def rglru_mix(proj, conv_w, conv_b, r_w, r_b, i_w, i_b, lru_L):
    bsz, seq, _ = proj.shape
    gate_branch, xb = jnp.split(proj, 2, axis=-1)
    xb = causal_depthwise_conv(xb, conv_w, conv_b)
    xh = xb.reshape(bsz, seq, LRU_BLOCKS, LRU_BLOCK)
    r = jax.nn.sigmoid((jnp.einsum('bsgi,gij->bsgj', xh, r_w).reshape(bsz, seq, LRU_WIDTH) + r_b).astype(jnp.float32))
    i = jax.nn.sigmoid((jnp.einsum('bsgi,gij->bsgj', xh, i_w).reshape(bsz, seq, LRU_WIDTH) + i_b).astype(jnp.float32))
    log_a = -LRU_C * r * jax.nn.softplus(-lru_L.astype(jnp.float32))
    a = jnp.exp(log_a)
    b = jnp.sqrt(-jnp.expm1(2.0 * log_a)) * (i * xb.astype(jnp.float32))
    _, h = lax.associative_scan(_linear_recurrence_combine, (a, b), axis=1)
    return (h * jax.nn.gelu(gate_branch.astype(jnp.float32))).astype(proj.dtype)


def peer_ffn(x, w_q, subkeys, u_tab, v_tab):
    bsz, seq, d = x.shape
    xt = x.reshape(-1, d)
    q = (xt @ w_q).reshape(-1, PEER_HEADS, 2, PEER_HALF)
    s1 = jnp.einsum('thd,hnd->thn', q[:, :, 0], subkeys[:, 0])
    s2 = jnp.einsum('thd,hnd->thn', q[:, :, 1], subkeys[:, 1])
    v1, i1 = lax.top_k(s1, PEER_TOPK)
    v2, i2 = lax.top_k(s2, PEER_TOPK)
    cand = (v1[..., :, None] + v2[..., None, :]).reshape(-1, PEER_HEADS, PEER_TOPK * PEER_TOPK)
    cand_id = (i1[..., :, None] * PEER_N_KEYS + i2[..., None, :]).reshape(-1, PEER_HEADS, PEER_TOPK * PEER_TOPK)
    top_s, pos = lax.top_k(cand, PEER_TOPK)
    ids = jnp.take_along_axis(cand_id, pos, axis=-1)
    gates = jax.nn.softmax(top_s.astype(jnp.float32), axis=-1).astype(x.dtype)
    nb = xt.shape[0] // PEER_TOKEN_BLOCK

    def expert_block(args):
        xb, idb, gb = args
        u = jnp.take(u_tab, idb, axis=0)
        v = jnp.take(v_tab, idb, axis=0)
        act = jax.nn.gelu(jnp.einsum('thkd,td->thk', u, xb))
        return jnp.einsum('thk,thkd->td', gb * act, v)

    y = lax.map(expert_block, (xt.reshape(nb, PEER_TOKEN_BLOCK, d),
                               ids.reshape(nb, PEER_TOKEN_BLOCK, PEER_HEADS, PEER_TOPK),
                               gates.reshape(nb, PEER_TOKEN_BLOCK, PEER_HEADS, PEER_TOPK)))
    return y.reshape(bsz, seq, d)


def setup_inputs(seed: int = 0) -> dict:
    key = jax.random.key(seed)
    ks = iter(jax.random.split(key, 48))
    f32 = jnp.float32

    def nrm(shape, scale):
        return jax.random.normal(next(ks), shape, f32) * scale

    def unif(shape, lo, hi):
        return jax.random.uniform(next(ks), shape, f32, lo, hi)

    NE, NO, L, D = N_EVEN, N_ODD, DEPTH, D_MODEL
    x = nrm((BATCH, SEQ, D), 1.0)
    p = nrm((DEPTH, BATCH, SEQ, PLE_DIM), 1.0)
    ev_w_in = nrm((NE, D, EV_IN_COLS), D ** -0.5)
    ev_shift_mu = unif((NE, RWKV_COLS), 0.0, 1.0)
    ev_rwkv_w0 = unif((NE, RWKV_WIDTH), -5.0, 0.5)
    ev_rwkv_w_up = nrm((NE, RWKV_DECAY_LORA, RWKV_WIDTH), 0.1)
    ev_rwkv_a0 = nrm((NE, RWKV_WIDTH), 0.1)
    ev_rwkv_a_up = nrm((NE, RWKV_AAA_LORA, RWKV_WIDTH), RWKV_AAA_LORA ** -0.5)
    ev_rwkv_g_up = nrm((NE, RWKV_GATE_LORA, RWKV_WIDTH), RWKV_GATE_LORA ** -0.5)
    ev_rwkv_k_k = 0.85 + nrm((NE, RWKV_WIDTH), 0.02)
    ev_rwkv_k_a = 1.0 + nrm((NE, RWKV_WIDTH), 0.02)
    ev_rwkv_r_k = nrm((NE, RWKV_HEADS, RWKV_HEAD), 0.1)
    ev_rwkv_ln_g = 1.0 + nrm((NE, RWKV_WIDTH), 0.02)
    ev_rwkv_ln_b = nrm((NE, RWKV_WIDTH), 0.02)
    ev_gdn_conv = nrm((NE, GDN_CONV, 3 * GDN_WIDTH), GDN_CONV ** -0.5)
    ev_gdn_a_log = jnp.log(unif((NE, GDN_HEADS), 1.0, 16.0))
    dt = jnp.exp(unif((NE, GDN_HEADS), float(np.log(1e-3)), float(np.log(1e-1))))
    ev_gdn_dt_bias = dt + jnp.log(-jnp.expm1(-dt))
    ev_gdn_norm_g = 1.0 + nrm((NE, GDN_HEAD), 0.02)
    ev_w_out = nrm((NE, D_MODEL, D), D_MODEL ** -0.5 * DN_BETA)
    od_w_in = nrm((NO, D, 2 * LRU_WIDTH), D ** -0.5)
    od_conv_w = nrm((NO, LRU_CONV, LRU_WIDTH), LRU_CONV ** -0.5)
    od_conv_b = nrm((NO, LRU_WIDTH), 0.02)
    od_r_w = nrm((NO, LRU_BLOCKS, LRU_BLOCK, LRU_BLOCK), LRU_BLOCK ** -0.5)
    od_r_b = nrm((NO, LRU_WIDTH), 0.02)
    od_i_w = nrm((NO, LRU_BLOCKS, LRU_BLOCK, LRU_BLOCK), LRU_BLOCK ** -0.5)
    od_i_b = nrm((NO, LRU_WIDTH), 0.02)
    a0 = unif((NO, LRU_WIDTH), 0.9, 0.999) ** (1.0 / LRU_C)
    od_lru_L = jnp.log(a0) - jnp.log1p(-a0)
    od_w_out = nrm((NO, LRU_WIDTH, D), LRU_WIDTH ** -0.5 * DN_BETA)
    ln_mix_g = 1.0 + nrm((L, D), 0.02)
    ln_mix_b = nrm((L, D), 0.02)
    peer_w_q = nrm((L, D, PEER_HEADS * PEER_KEY_DIM), D ** -0.5)
    peer_subkeys = nrm((L, PEER_HEADS, 2, PEER_N_KEYS, PEER_HALF), PEER_HALF ** -0.5)
    peer_u = nrm((L, PEER_N_EXPERTS, D), D ** -0.5)
    peer_v = nrm((L, PEER_N_EXPERTS, D), DN_BETA)
    ln_ffn_g = 1.0 + nrm((L, D), 0.02)
    ln_ffn_b = nrm((L, D), 0.02)
    ple_w_proj = nrm((L, PLE_DIM, D), PLE_DIM ** -0.5 * 0.5)
    ple_w_gate = nrm((L, D, D), D ** -0.5)
    ple_b_gate = nrm((L, D), 0.02)
    return {'x': x, 'p': p, 'ev_w_in': ev_w_in, 'ev_shift_mu': ev_shift_mu,
            'ev_rwkv_w0': ev_rwkv_w0, 'ev_rwkv_w_up': ev_rwkv_w_up, 'ev_rwkv_a0': ev_rwkv_a0,
            'ev_rwkv_a_up': ev_rwkv_a_up, 'ev_rwkv_g_up': ev_rwkv_g_up, 'ev_rwkv_k_k': ev_rwkv_k_k,
            'ev_rwkv_k_a': ev_rwkv_k_a, 'ev_rwkv_r_k': ev_rwkv_r_k, 'ev_rwkv_ln_g': ev_rwkv_ln_g,
            'ev_rwkv_ln_b': ev_rwkv_ln_b, 'ev_gdn_conv': ev_gdn_conv, 'ev_gdn_a_log': ev_gdn_a_log,
            'ev_gdn_dt_bias': ev_gdn_dt_bias, 'ev_gdn_norm_g': ev_gdn_norm_g, 'ev_w_out': ev_w_out,
            'od_w_in': od_w_in, 'od_conv_w': od_conv_w, 'od_conv_b': od_conv_b, 'od_r_w': od_r_w,
            'od_r_b': od_r_b, 'od_i_w': od_i_w, 'od_i_b': od_i_b, 'od_lru_L': od_lru_L,
            'od_w_out': od_w_out, 'ln_mix_g': ln_mix_g, 'ln_mix_b': ln_mix_b, 'peer_w_q': peer_w_q,
            'peer_subkeys': peer_subkeys, 'peer_u': peer_u, 'peer_v': peer_v, 'ln_ffn_g': ln_ffn_g,
            'ln_ffn_b': ln_ffn_b, 'ple_w_proj': ple_w_proj, 'ple_w_gate': ple_w_gate,
            'ple_b_gate': ple_b_gate}


def reference(x, p, ev_w_in, ev_shift_mu, ev_rwkv_w0, ev_rwkv_w_up, ev_rwkv_a0, ev_rwkv_a_up,
              ev_rwkv_g_up, ev_rwkv_k_k, ev_rwkv_k_a, ev_rwkv_r_k, ev_rwkv_ln_g, ev_rwkv_ln_b,
              ev_gdn_conv, ev_gdn_a_log, ev_gdn_dt_bias, ev_gdn_norm_g, ev_w_out,
              od_w_in, od_conv_w, od_conv_b, od_r_w, od_r_b, od_i_w, od_i_b, od_lru_L, od_w_out,
              ln_mix_g, ln_mix_b, peer_w_q, peer_subkeys, peer_u, peer_v, ln_ffn_g, ln_ffn_b,
              ple_w_proj, ple_w_gate, ple_b_gate):
    for layer in range(DEPTH):
        j = layer // 2
        if layer % 2 == 0:
            proj = x @ ev_w_in[j]
            y_a = rwkv7_time_mix(proj[..., :RWKV_COLS], ev_shift_mu[j], ev_rwkv_w0[j], ev_rwkv_w_up[j],
                                 ev_rwkv_a0[j], ev_rwkv_a_up[j], ev_rwkv_g_up[j], ev_rwkv_k_k[j],
                                 ev_rwkv_k_a[j], ev_rwkv_r_k[j], ev_rwkv_ln_g[j], ev_rwkv_ln_b[j])
            y_b = gated_deltanet_mix(proj[..., RWKV_COLS:], ev_gdn_conv[j], ev_gdn_a_log[j],
                                     ev_gdn_dt_bias[j], ev_gdn_norm_g[j])
            mix = jnp.concatenate([y_a, y_b], axis=-1) @ ev_w_out[j]
        else:
            mix = rglru_mix(x @ od_w_in[j], od_conv_w[j], od_conv_b[j], od_r_w[j], od_r_b[j],
                            od_i_w[j], od_i_b[j], od_lru_L[j]) @ od_w_out[j]
        x = layer_norm(DN_ALPHA * x + mix, ln_mix_g[layer], ln_mix_b[layer])
        ffn = peer_ffn(x, peer_w_q[layer], peer_subkeys[layer], peer_u[layer], peer_v[layer])
        x = layer_norm(DN_ALPHA * x + ffn, ln_ffn_g[layer], ln_ffn_b[layer])
        x = x + jax.nn.sigmoid(x @ ple_w_gate[layer] + ple_b_gate[layer]) * (p[layer] @ ple_w_proj[layer])
    return x
```

```python
import functools

import jax
import jax.numpy as jnp
from jax import lax
from jax.experimental import pallas as pl
from jax.experimental.pallas import tpu as pltpu

F32 = jnp.float32
BF16 = jnp.bfloat16

D_MODEL = 2048
DEPTH = 2
DN_ALPHA = (2.0 * DEPTH) ** 0.25
LN_EPS = 1e-5

RWKV_WIDTH = 1024
RWKV_HEAD = 64
RWKV_GN_EPS = 64e-5
RWKV_CHUNK = 64
GDN_WIDTH = 1024
GDN_HEAD = 128
GDN_HEADS = 8
GDN_CHUNK = 128
GDN_NORM_EPS = 1e-6
LRU_BLOCKS = 8
LRU_BLOCK = 256
LRU_C = 8.0
PEER_HEADS = 8
PEER_KEYS = 128
PEER_TOPK = 16

LANES = 128
SUBLANES = 8
VMEM_LIMIT = 56 * 1024 * 1024

EV_QKV0 = 0
EV_RKV0 = 3072
EV_Z0 = 6144
EV_LORA0 = 7168
EV_BA0 = 7552
EV_COLS = 7680


def _cparams(n_axes):
    return pltpu.CompilerParams(dimension_semantics=("arbitrary",) * n_axes,
                                vmem_limit_bytes=VMEM_LIMIT)


def _dot(a, b):
    return jnp.dot(a.astype(BF16), b.astype(BF16), preferred_element_type=F32)


def _dot_nt(a, b):
    return lax.dot_general(a.astype(BF16), b.astype(BF16), (((1,), (1,)), ((), ())),
                           preferred_element_type=F32)


def _dot_tn(a, b):
    return _dot(a.T, b)


def _split(x):
    hi = x.astype(BF16)
    lo = (x - hi.astype(F32)).astype(BF16)
    return hi, lo


def _dot_lhs2(a, b):
    hi, lo = _split(a)
    return _dot(hi, b) + _dot(lo, b)


def _dot_rhs2(a, b):
    hi, lo = _split(b)
    return _dot(a, hi) + _dot(a, lo)


def _dot3(a, b):
    ah, al = _split(a)
    bh, bl = _split(b)
    return _dot(ah, bh) + _dot(al, bh) + _dot(ah, bl)


def _sigmoid(x):
    return 1.0 / (1.0 + jnp.exp(-x))


def _softplus(x):
    return jnp.maximum(x, 0.0) + jnp.log(1.0 + jnp.exp(-jnp.abs(x)))


def _gelu(x):
    return 0.5 * x * (1.0 + jnp.tanh(0.7978845608028654 * (x + 0.044715 * (x * x * x))))


def _iota2(shape, axis):
    return lax.broadcasted_iota(jnp.int32, shape, axis)


def _layer_norm(x, g, b):
    mu = jnp.mean(x, axis=-1, keepdims=True)
    xc = x - mu
    var = jnp.mean(xc * xc, axis=-1, keepdims=True)
    return xc * lax.rsqrt(var + LN_EPS) * g + b


def _shift_rows(x, halo, k):
    rolled = pltpu.roll(x, k, 0)
    hr = pltpu.roll(halo, k, 0)
    row = _iota2(halo.shape, 0)
    top = jnp.where(row < k, hr, rolled[0:SUBLANES])
    return jnp.concatenate([top, rolled[SUBLANES:]], axis=0)


def _tri_inv(x, n, top):
    row = _iota2((n, n), 0)
    col = _iota2((n, n), 1)
    eye = (row == col).astype(F32)
    x0 = jnp.where((row >> 3) == (col >> 3), x, 0.0)
    x2 = _dot(x0, x0)
    x4 = _dot(x2, x2)
    t = eye + x0
    t = t + _dot(t, x2)
    t = t + _dot(t, x4)
    s = 8
    while s < top:
        sh = s.bit_length() - 1
        off = ((row >> (sh + 1)) == (col >> (sh + 1))) & ((row >> sh) != (col >> sh))
        xo = jnp.where(off, x, 0.0)
        t = t + _dot(_dot(t, xo), t)
        s *= 2
    return t


def _mm_kernel(x_ref, w_ref, o_ref):
    o_ref[...] = jnp.dot(x_ref[...].astype(BF16), w_ref[...],
                         preferred_element_type=F32).astype(o_ref.dtype)


def _matmul(x, w, *, tm, tn, out_dtype=F32):
    m, k = x.shape
    n = w.shape[1]
    return pl.pallas_call(
        _mm_kernel, grid=(m // tm, n // tn),
        in_specs=[pl.BlockSpec((tm, k), lambda i, j: (i, 0)),
                  pl.BlockSpec((k, tn), lambda i, j: (0, j))],
        out_specs=pl.BlockSpec((tm, tn), lambda i, j: (i, j)),
        out_shape=jax.ShapeDtypeStruct((m, n), out_dtype),
        compiler_params=_cparams(2), name="matmul")(x, w)


def _proj_ln_kernel(*refs, n_in):
    ys = refs[:n_in]
    ws = refs[n_in:2 * n_in]
    x_ref, g_ref, b_ref, o_ref, ob_ref = refs[2 * n_in:]
    mix = _dot(ys[0][...], ws[0][...])
    for y_ref, w_ref in zip(ys[1:], ws[1:]):
        mix = mix + _dot(y_ref[...], w_ref[...])
    xn = _layer_norm(DN_ALPHA * x_ref[...] + mix, g_ref[...], b_ref[...])
    o_ref[...] = xn
    ob_ref[...] = xn.astype(BF16)


def _proj_ln(ys, ws, x, g, b, *, tm):
    t, d = x.shape
    n_in = len(ys)
    in_specs = ([pl.BlockSpec((tm, y.shape[1]), lambda i: (i, 0)) for y in ys]
                + [pl.BlockSpec(w.shape, lambda i: (0, 0)) for w in ws]
                + [pl.BlockSpec((tm, d), lambda i: (i, 0)),
                   pl.BlockSpec((1, d), lambda i: (0, 0)),
                   pl.BlockSpec((1, d), lambda i: (0, 0))])
    return pl.pallas_call(
        functools.partial(_proj_ln_kernel, n_in=n_in), grid=(t // tm,),
        in_specs=in_specs,
        out_specs=[pl.BlockSpec((tm, d), lambda i: (i, 0)), pl.BlockSpec((tm, d), lambda i: (i, 0))],
        out_shape=[jax.ShapeDtypeStruct((t, d), F32), jax.ShapeDtypeStruct((t, d), BF16)],
        compiler_params=_cparams(1), name="proj_ln")(*ys, *ws, x, g.reshape(1, d), b.reshape(1, d))


def _rwkv_pre_kernel(p_ref, ph_ref, l_ref, lh_ref, mu_ref, mul_ref, w0_ref, wup_ref, a0_ref, aup_ref,
                     gup_ref, kk_ref, ka_ref, rk_ref, bd_ref,
                     r_o, lw_o, k_o, v_o, a_o, b_o, bonus_o, g_o, *, tiles_per_seq):
    first = (pl.program_id(0) % tiles_per_seq) == 0
    w = RWKV_WIDTH

    def mixed(ref, href, m_ref, c0, c1):
        x = ref[:, c0:c1]
        hrow = jnp.where(first, 0.0, href[SUBLANES - 1:SUBLANES, c0:c1])
        prev = pltpu.roll(x, 1, 0)
        prev = jnp.where(_iota2(x.shape, 0) == 0, hrow, prev)
        return x + (prev - x) * m_ref[:, c0:c1]

    def segsum(x):
        bd = bd_ref[...]
        return jnp.concatenate(
            [_dot_lhs2(x[:, LANES * p:LANES * (p + 1)], bd) for p in range(w // LANES)], axis=1)

    r = mixed(p_ref, ph_ref, mu_ref, 0, w)
    k = mixed(p_ref, ph_ref, mu_ref, w, 2 * w)
    v = mixed(p_ref, ph_ref, mu_ref, 2 * w, 3 * w)
    wd = mixed(l_ref, lh_ref, mul_ref, 0, LANES)
    ad = mixed(l_ref, lh_ref, mul_ref, LANES, 2 * LANES)
    gd = mixed(l_ref, lh_ref, mul_ref, 2 * LANES, 3 * LANES)

    w_log = -_softplus(-(w0_ref[...] + _dot3(jnp.tanh(wd), wup_ref[...]))) - 0.5
    lw_o[...] = -jnp.exp(w_log)
    a = _sigmoid(a0_ref[...] + _dot3(ad, aup_ref[...]))
    g_o[...] = _dot3(_sigmoid(gd), gup_ref[...])
    kkr = k * kk_ref[...]
    kk = kkr * lax.rsqrt(segsum(kkr * kkr) + 1e-6)
    k2 = k * (1.0 + (a - 1.0) * ka_ref[...])
    bonus_o[...] = segsum(r * k2 * rk_ref[...]) * v
    r_o[...] = r
    k_o[...] = k2
    v_o[...] = v
    a_o[...] = -kk
    b_o[...] = kk * a


def _rwkv_pre(proj, seq, mu_rkv, mu_lora, w0, w_up, a0, a_up, g_up, k_k, k_a, r_k, bd64, *, tm):
    t = proj.shape[0]
    w = RWKV_WIDTH
    nh = tm // SUBLANES
    row = lambda a: a.reshape(1, -1)
    full = lambda a: pl.BlockSpec(a.shape, lambda i: (0, 0))
    args = [row(mu_rkv), row(mu_lora), row(w0), w_up, row(a0), a_up, g_up, row(k_k), row(k_a), row(r_k), bd64]
    in_specs = [pl.BlockSpec((tm, 3 * w), lambda i: (i, EV_RKV0 // (3 * w))),
                pl.BlockSpec((SUBLANES, 3 * w), lambda i: (jnp.maximum(i * nh - 1, 0), EV_RKV0 // (3 * w))),
                pl.BlockSpec((tm, 512), lambda i: (i, EV_LORA0 // 512)),
                pl.BlockSpec((SUBLANES, 512), lambda i: (jnp.maximum(i * nh - 1, 0), EV_LORA0 // 512))]
    in_specs += [full(a) for a in args]
    out_spec = pl.BlockSpec((tm, w), lambda i: (i, 0))
    return pl.pallas_call(
        functools.partial(_rwkv_pre_kernel, tiles_per_seq=seq // tm), grid=(t // tm,),
        in_specs=in_specs, out_specs=[out_spec] * 8,
        out_shape=[jax.ShapeDtypeStruct((t, w), F32)] * 8,
        compiler_params=_cparams(1), name="rwkv_pre")(proj, proj, proj, proj, *args)


def _rwkv_chunk_kernel(r_ref, lw_ref, k_ref, v_ref, a_ref, b_ref, bonus_ref, g_ref, lng_ref, lnb_ref,
                       bd_ref, o_ref, z_ref, *, n_chunks):
    L = RWKV_CHUNK

    @pl.when(pl.program_id(2) == 0)
    def _():
        z_ref[...] = jnp.zeros_like(z_ref)

    lane = _iota2((L, LANES), 1)
    head0 = lane < RWKV_HEAD
    row = _iota2((2 * L, 2 * L), 0)
    col = _iota2((2 * L, 2 * L), 1)
    strict = row > col
    incl = row >= col
    eye = row == col
    tril = (_iota2((L, L), 0) >= _iota2((L, L), 1)).astype(BF16)
    bd = bd_ref[...]

    def stack2(x):
        return jnp.concatenate([jnp.where(head0, x, 0.0), jnp.where(head0, 0.0, x)], axis=0)

    def body(c, carry):
        sl = pl.ds(pl.multiple_of(c * L, L), L)
        r, lw, k, v, a, b = (ref[sl, :] for ref in (r_ref, lw_ref, k_ref, v_ref, a_ref, b_ref))
        cl = _dot_rhs2(tril, lw)
        cl_last = cl[L - 1:L, :]
        e_in = jnp.exp(cl)
        e_out = jnp.exp(-cl)
        e_end = jnp.exp(cl_last - cl)
        rt = stack2(r * e_in)
        at = stack2(a * jnp.exp(cl - lw))
        kt = stack2(k * e_out)
        bt = stack2(b * e_out)
        kc = stack2(k * e_end)
        bc = stack2(b * e_end)
        vs = stack2(v)
        aa = _dot_nt(jnp.concatenate([at, rt], axis=0), jnp.concatenate([bt, kt], axis=0))
        a_ab = jnp.where(strict, aa[0:2 * L, 0:2 * L], 0.0)
        a_ak = jnp.where(strict, aa[0:2 * L, 2 * L:4 * L], 0.0)
        a_rb = jnp.where(incl, aa[2 * L:4 * L, 0:2 * L], 0.0)
        a_rk = jnp.where(incl, aa[2 * L:4 * L, 2 * L:4 * L], 0.0)
        tm = _tri_inv(a_ab, 2 * L, L)
        a_hat = _dot(tm, at)
        u_v = _dot(tm, _dot(a_ak, vs))
        r_hat = rt + _dot(a_rb, a_hat)
        y_v = _dot(a_rb, u_v) + _dot(a_rk, vs)
        m = jnp.where(eye, jnp.exp(cl_last), 0.0) + _dot_tn(bc, a_hat)
        n = _dot_tn(bc, u_v) + _dot_tn(kc, vs)
        z = z_ref[...]
        ys = _dot(r_hat, z) + y_v
        z_ref[...] = _dot(m, z) + n
        y = ys[0:L] + ys[L:2 * L]
        mean = _dot_lhs2(y, bd) * (1.0 / RWKV_HEAD)
        yc = y - mean
        var = _dot_lhs2(yc * yc, bd) * (1.0 / RWKV_HEAD)
        yn = yc * lax.rsqrt(var + RWKV_GN_EPS) * lng_ref[...] + lnb_ref[...]
        o_ref[sl, :] = ((yn + bonus_ref[sl, :]) * g_ref[sl, :]).astype(o_ref.dtype)
        return carry

    lax.fori_loop(0, n_chunks, body, 0)


def _rwkv_chunks(r, lw, k, v, a, b, bonus, g, ln_g, ln_b, bd64, *, batch, seq, tb):
    t, w = r.shape
    npairs = w // LANES
    nblk = seq // tb
    tok = pl.BlockSpec((tb, LANES), lambda bi, p, j: (bi * nblk + j, p))
    par = pl.BlockSpec((1, LANES), lambda bi, p, j: (0, p))
    return pl.pallas_call(
        functools.partial(_rwkv_chunk_kernel, n_chunks=tb // RWKV_CHUNK),
        grid=(batch, npairs, nblk),
        in_specs=[tok] * 8 + [par, par, pl.BlockSpec((LANES, LANES), lambda bi, p, j: (0, 0))],
        out_specs=tok,
        out_shape=jax.ShapeDtypeStruct((t, w), BF16),
        scratch_shapes=[pltpu.VMEM((LANES, LANES), F32)],
        compiler_params=_cparams(3), name="rwkv_chunks")(
            r, lw, k, v, a, b, bonus, g, ln_g.reshape(1, w), ln_b.reshape(1, w), bd64)


def _gdn_pre_kernel(p_ref, ph_ref, ba_ref, cw_ref, alog_ref, dtb_ref,
                    q_o, k_o, kb_o, vb_o, gb_o, *, tiles_per_seq):
    first = (pl.program_id(0) % tiles_per_seq) == 0
    w = GDN_WIDTH
    tm = p_ref.shape[0]
    ones = jnp.ones((LANES, LANES), BF16)

    def conv_silu(c0, c1):
        x = p_ref[:, c0:c1]
        halo = jnp.where(first, 0.0, ph_ref[:, c0:c1])
        out = x * cw_ref[3:4, c0:c1]
        for kshift in (1, 2, 3):
            out = out + _shift_rows(x, halo, kshift) * cw_ref[3 - kshift:4 - kshift, c0:c1]
        return out * _sigmoid(out)

    ba = ba_ref[...]
    beta = _sigmoid(ba)
    glog = -jnp.exp(alog_ref[...]) * _softplus(ba + dtb_ref[...])
    srow = _iota2((LANES, LANES), 0)
    for h in range(GDN_HEADS):
        sl = slice(GDN_HEAD * h, GDN_HEAD * (h + 1))
        beta_b = _dot_lhs2(beta, (srow == h).astype(BF16))
        gb_o[:, sl] = _dot_lhs2(glog, (srow == GDN_HEADS + h).astype(BF16))
        q = conv_silu(GDN_HEAD * h, GDN_HEAD * (h + 1))
        k = conv_silu(w + GDN_HEAD * h, w + GDN_HEAD * (h + 1))
        v = conv_silu(2 * w + GDN_HEAD * h, 2 * w + GDN_HEAD * (h + 1))
        qn = q * lax.rsqrt(_dot_lhs2(q * q, ones) + 1e-6) * (GDN_HEAD ** -0.5)
        kn = k * lax.rsqrt(_dot_lhs2(k * k, ones) + 1e-6)
        q_o[:, sl] = qn.astype(BF16)
        k_o[:, sl] = kn.astype(BF16)
        kb_o[:, sl] = (kn * beta_b).astype(BF16)
        vb_o[:, sl] = (v * beta_b).astype(BF16)
    del tm


def _gdn_pre(proj, seq, conv_w, a_log, dt_bias, *, tm):
    t = proj.shape[0]
    w = GDN_WIDTH
    nh = tm // SUBLANES
    pad = lambda a: jnp.zeros((1, LANES), F32).at[0, GDN_HEADS:2 * GDN_HEADS].set(a)
    full = lambda a: pl.BlockSpec(a.shape, lambda i: (0, 0))
    args = [conv_w, pad(a_log), pad(dt_bias)]
    out_spec = pl.BlockSpec((tm, w), lambda i: (i, 0))
    return pl.pallas_call(
        functools.partial(_gdn_pre_kernel, tiles_per_seq=seq // tm), grid=(t // tm,),
        in_specs=[pl.BlockSpec((tm, 3 * w), lambda i: (i, EV_QKV0 // (3 * w))),
                  pl.BlockSpec((SUBLANES, 3 * w), lambda i: (jnp.maximum(i * nh - 1, 0), EV_QKV0 // (3 * w))),
                  pl.BlockSpec((tm, LANES), lambda i: (i, EV_BA0 // LANES))] + [full(a) for a in args],
        out_specs=[out_spec] * 5,
        out_shape=[jax.ShapeDtypeStruct((t, w), BF16)] * 4 + [jax.ShapeDtypeStruct((t, w), F32)],
        compiler_params=_cparams(1), name="gdn_pre")(proj, proj, proj, *args)


def _gdn_chunk_kernel(q_ref, k_ref, kb_ref, vb_ref, gb_ref, z_ref, ng_ref, o_ref, s_ref, *, n_chunks):
    C = GDN_CHUNK

    @pl.when(pl.program_id(2) == 0)
    def _():
        s_ref[...] = jnp.zeros_like(s_ref)

    row = _iota2((C, C), 0)
    col = _iota2((C, C), 1)
    strict = row > col
    incl = row >= col
    eye = row == col
    tril = incl.astype(BF16)

    def body(c, carry):
        sl = pl.ds(pl.multiple_of(c * C, C), C)
        q, k, kb, vb = (ref[sl, :].astype(F32) for ref in (q_ref, k_ref, kb_ref, vb_ref))
        gc = _dot_rhs2(tril, gb_ref[sl, :])
        gct = gc.T
        dm = jnp.exp(jnp.where(incl, gc - gct, -1e30))
        egc = jnp.exp(gc)
        g_last = gc[C - 1:C, :]
        kk = _dot_nt(kb, k)
        qk = _dot_nt(q, k)
        m_mat = jnp.where(strict, kk * dm, 0.0)
        attn = qk * dm
        tinv = _tri_inv(-m_mat, C, C)
        u = _dot(tinv, vb)
        w = _dot(tinv, kb * egc)
        q_hat = q * egc - _dot(attn, w)
        o_v = _dot(attn, u)
        kd = k * jnp.exp(g_last - gc)
        mm = jnp.where(eye, jnp.exp(g_last), 0.0) - _dot_tn(kd, w)
        nn = _dot_tn(kd, u)
        s = s_ref[...]
        o = _dot(q_hat, s) + o_v
        s_ref[...] = _dot(mm, s) + nn
        o = o * lax.rsqrt(jnp.mean(o * o, axis=-1, keepdims=True) + GDN_NORM_EPS) * ng_ref[...]
        zz = z_ref[sl, :]
        o_ref[sl, :] = (o * (zz * _sigmoid(zz))).astype(o_ref.dtype)
        return carry

    lax.fori_loop(0, n_chunks, body, 0)


def _gdn_chunks(q, k, kb, vb, gb, proj, norm_g, *, batch, seq, tb):
    t, w = q.shape
    nblk = seq // tb
    tok = pl.BlockSpec((tb, GDN_HEAD), lambda bi, h, j: (bi * nblk + j, h))
    zspec = pl.BlockSpec((tb, GDN_HEAD), lambda bi, h, j: (bi * nblk + j, EV_Z0 // GDN_HEAD + h))
    return pl.pallas_call(
        functools.partial(_gdn_chunk_kernel, n_chunks=tb // GDN_CHUNK),
        grid=(batch, GDN_HEADS, nblk),
        in_specs=[tok] * 5 + [zspec, pl.BlockSpec((1, GDN_HEAD), lambda bi, h, j: (0, 0))],
        out_specs=tok,
        out_shape=jax.ShapeDtypeStruct((t, w), BF16),
        scratch_shapes=[pltpu.VMEM((GDN_HEAD, GDN_HEAD), F32)],
        compiler_params=_cparams(3), name="gdn_chunks")(q, k, kb, vb, gb, proj, norm_g.reshape(1, GDN_HEAD))


def _head_block_ones():
    i = jnp.arange(LANES)
    return ((i[:, None] // RWKV_HEAD) == (i[None, :] // RWKV_HEAD)).astype(BF16)


def _pad_rows(a, rows):
    return jnp.zeros((rows,) + a.shape[1:], a.dtype).at[:a.shape[0]].set(a)


def _even_mixers(proj, batch, seq, prm):
    bd64 = _head_block_ones()
    mu = prm['ev_shift_mu']
    w = RWKV_WIDTH
    mu_lora = jnp.zeros((512,), F32)
    mu_lora = mu_lora.at[0:96].set(mu[3 * w:3 * w + 96]).at[128:224].set(mu[3 * w + 96:3 * w + 192])
    mu_lora = mu_lora.at[256:320].set(mu[3 * w + 192:3 * w + 256])
    pre = _rwkv_pre(proj, seq, mu[:3 * w], mu_lora, prm['ev_rwkv_w0'], _pad_rows(prm['ev_rwkv_w_up'], LANES),
                    prm['ev_rwkv_a0'], _pad_rows(prm['ev_rwkv_a_up'], LANES),
                    _pad_rows(prm['ev_rwkv_g_up'], LANES), prm['ev_rwkv_k_k'], prm['ev_rwkv_k_a'],
                    prm['ev_rwkv_r_k'].reshape(-1), bd64, tm=256)
    y_a = _rwkv_chunks(*pre, prm['ev_rwkv_ln_g'], prm['ev_rwkv_ln_b'], bd64, batch=batch, seq=seq, tb=256)
    gpre = _gdn_pre(proj, seq, prm['ev_gdn_conv'], prm['ev_gdn_a_log'], prm['ev_gdn_dt_bias'], tm=256)
    y_b = _gdn_chunks(*gpre, proj, prm['ev_gdn_norm_g'], batch=batch, seq=seq, tb=256)
    return y_a, y_b


def _lru_kernel(x_ref, xh_ref, gate_ref, cw_ref, cb_ref, rw_ref, rb_ref, iw_ref, ib_ref, l_ref,
                o_ref, h_ref):
    j = pl.program_id(2)
    tm = x_ref.shape[0]

    @pl.when(j == 0)
    def _():
        h_ref[...] = jnp.zeros_like(h_ref)

    x = x_ref[...]
    halo = jnp.where(j == 0, 0.0, xh_ref[...])
    xc = x * cw_ref[3:4, :] + cb_ref[...]
    for kshift in (1, 2, 3):
        xc = xc + _shift_rows(x, halo, kshift) * cw_ref[3 - kshift:4 - kshift, :]
    r = _sigmoid(_dot(xc, rw_ref[0]) + rb_ref[...])
    i = _sigmoid(_dot(xc, iw_ref[0]) + ib_ref[...])
    log_a = -LRU_C * r * _softplus(-l_ref[...])
    a = jnp.exp(log_a)
    b = jnp.sqrt(1.0 - jnp.exp(2.0 * log_a)) * (i * xc)
    row = _iota2(a.shape, 0)
    s = 1
    while s < tm:
        keep = row >= s
        a_sh = jnp.where(keep, pltpu.roll(a, s, 0), 1.0)
        b_sh = jnp.where(keep, pltpu.roll(b, s, 0), 0.0)
        b = a * b_sh + b
        a = a * a_sh
        s *= 2
    h = b + a * h_ref[0:1, :]
    h_ref[...] = jnp.broadcast_to(h[tm - 1:tm, :], h_ref.shape)
    o_ref[...] = (h * _gelu(gate_ref[...])).astype(o_ref.dtype)


def _lru(proj, conv_w, conv_b, r_w, r_b, i_w, i_b, lru_l, *, batch, seq, tm):
    t = proj.shape[0]
    w = LRU_BLOCKS * LRU_BLOCK
    nblk = seq // tm
    nh = tm // SUBLANES
    vec = pl.BlockSpec((1, LRU_BLOCK), lambda bi, g, j: (0, g))
    mat = pl.BlockSpec((1, LRU_BLOCK, LRU_BLOCK), lambda bi, g, j: (g, 0, 0))
    row = lambda a: a.reshape(1, w)
    return pl.pallas_call(
        _lru_kernel, grid=(batch, LRU_BLOCKS, nblk),
        in_specs=[pl.BlockSpec((tm, LRU_BLOCK), lambda bi, g, j: (bi * nblk + j, LRU_BLOCKS + g)),
                  pl.BlockSpec((SUBLANES, LRU_BLOCK),
                               lambda bi, g, j: (jnp.maximum((bi * nblk + j) * nh - 1, 0), LRU_BLOCKS + g)),
                  pl.BlockSpec((tm, LRU_BLOCK), lambda bi, g, j: (bi * nblk + j, g)),
                  pl.BlockSpec((4, LRU_BLOCK), lambda bi, g, j: (0, g)), vec, mat, vec, mat, vec, vec],
        out_specs=pl.BlockSpec((tm, LRU_BLOCK), lambda bi, g, j: (bi * nblk + j, g)),
        out_shape=jax.ShapeDtypeStruct((t, w), BF16),
        scratch_shapes=[pltpu.VMEM((SUBLANES, LRU_BLOCK), F32)],
        compiler_params=_cparams(3), name="rglru")(
            proj, proj, proj, conv_w, row(conv_b), r_w.astype(BF16), row(r_b), i_w.astype(BF16), row(i_b),
            row(lru_l))


def _peer_topk_kernel(q_ref, sk_ref, id_o, gate_o, v_ref, i_ref, s_ref, c_ref):
    tm = q_ref.shape[0]
    kiota = _iota2((PEER_KEYS, tm), 0).astype(F32)
    neg = -jnp.inf

    for half in range(2):
        qh = q_ref[:, PEER_KEYS * half:PEER_KEYS * (half + 1)]
        ah, al = _split(sk_ref[0, half])
        bh, bl = _split(qh)
        nt = lambda a, b: lax.dot_general(a, b, (((1,), (1,)), ((), ())), preferred_element_type=F32)
        scores = nt(ah, bh) + nt(al, bh) + nt(ah, bl)

        def pick(kk, s, half=half):
            m = jnp.max(s, axis=0, keepdims=True)
            idx = jnp.min(jnp.where(s == m, kiota, 1e9), axis=0, keepdims=True)
            v_ref[half, pl.ds(kk, 1), :] = m
            i_ref[half, pl.ds(kk, 1), :] = idx
            return jnp.where(kiota == idx, neg, s)

        lax.fori_loop(0, PEER_TOPK, pick, scores)

    v1, v2 = v_ref[0], v_ref[1]
    i1, i2 = i_ref[0], i_ref[1]
    cands = [v1[0:1] + v2]
    ids = [i1[0:1] * PEER_KEYS + i2]
    for a in range(1, 8):
        cands.append(v1[a:a + 1] + v2[0:8])
        ids.append(i1[a:a + 1] * PEER_KEYS + i2[0:8])
    cands.append(v1[8:16] + v2[0:1])
    ids.append(i1[8:16] * PEER_KEYS + i2[0:1])
    cand = jnp.concatenate(cands, axis=0)
    c_ref[...] = jnp.concatenate(ids, axis=0)
    piota = _iota2(cand.shape, 0).astype(F32)

    def pick2(kk, c):
        m = jnp.max(c, axis=0, keepdims=True)
        pos = jnp.min(jnp.where(c == m, piota, 1e9), axis=0, keepdims=True)
        hit = piota == pos
        s_ref[pl.ds(kk, 1), :] = m
        id_o[0, pl.ds(kk, 1), :] = jnp.max(jnp.where(hit, c_ref[...], -1.0), axis=0, keepdims=True)
        return jnp.where(hit, neg, c)

    lax.fori_loop(0, PEER_TOPK, pick2, cand)
    top = s_ref[...]
    e = jnp.exp(top - top[0:1])
    gate_o[0] = e / jnp.sum(e, axis=0, keepdims=True)


def _peer_topk(q, subkeys, *, tm):
    t = q.shape[0]
    out = pl.BlockSpec((1, PEER_TOPK, tm), lambda i, h: (h, 0, i))
    return pl.pallas_call(
        _peer_topk_kernel, grid=(t // tm, PEER_HEADS),
        in_specs=[pl.BlockSpec((tm, 2 * PEER_KEYS), lambda i, h: (i, h)),
                  pl.BlockSpec((1, 2, PEER_KEYS, PEER_KEYS), lambda i, h: (h, 0, 0, 0))],
        out_specs=[out, out],
        out_shape=[jax.ShapeDtypeStruct((PEER_HEADS, PEER_TOPK, t), F32)] * 2,
        scratch_shapes=[pltpu.VMEM((2, PEER_TOPK, tm), F32), pltpu.VMEM((2, PEER_TOPK, tm), F32),
                        pltpu.VMEM((PEER_TOPK, tm), F32), pltpu.VMEM((80, tm), F32)],
        compiler_params=_cparams(2), name="peer_topk")(q, subkeys)


GATE_TILE = 64
GATE_STRIDE = GATE_TILE + SUBLANES


def _peer_gates_kernel(i1_ref, i2_ref, g_ref, o_ref, s_ref):
    ciota = _iota2((PEER_KEYS, LANES), 0).astype(F32)

    def body(t, carry):
        i1 = i1_ref[pl.ds(t, 1), :]
        i2 = i2_ref[pl.ds(t, 1), :]
        g = g_ref[pl.ds(t, 1), :]
        a = jnp.where(ciota == i1, g, 0.0)
        bt = jnp.where(ciota == i2, 1.0, 0.0)
        s_ref[pl.ds(t, PEER_KEYS, stride=GATE_STRIDE), :] = _dot_nt(a, bt)
        return carry

    lax.fori_loop(0, GATE_TILE, body, 0)
    for c in range(PEER_KEYS):
        o_ref[c] = s_ref[GATE_STRIDE * c:GATE_STRIDE * c + GATE_TILE, :].astype(o_ref.dtype)


def _peer_gates(i1, i2, gates):
    t = i1.shape[0]
    tok = pl.BlockSpec((GATE_TILE, LANES), lambda i: (i, 0))
    return pl.pallas_call(
        _peer_gates_kernel, grid=(t // GATE_TILE,),
        in_specs=[tok, tok, tok],
        out_specs=pl.BlockSpec((PEER_KEYS, GATE_TILE, PEER_KEYS), lambda i: (0, i, 0)),
        out_shape=jax.ShapeDtypeStruct((PEER_KEYS, t, PEER_KEYS), BF16),
        scratch_shapes=[pltpu.VMEM((PEER_KEYS * GATE_STRIDE, PEER_KEYS), F32)],
        compiler_params=_cparams(1), name="peer_gates")(i1, i2, gates)


PEER_STEP = 2 * PEER_KEYS


def _peer_ffn_kernel(x_ref, u_ref, v_ref, g_ref, o_ref):
    @pl.when(pl.program_id(1) == 0)
    def _():
        o_ref[...] = jnp.zeros_like(o_ref)

    h = lax.dot_general(x_ref[...], u_ref[...], (((1,), (1,)), ((), ())), preferred_element_type=F32)
    gate = jnp.concatenate([g_ref[0], g_ref[1]], axis=1).astype(F32)
    wgt = (_gelu(h) * gate).astype(BF16)
    o_ref[...] += jnp.dot(wgt, v_ref[...], preferred_element_type=F32)


def _peer_ffn(xb, u, v, gmat, *, tm):
    t, d = xb.shape
    ne = u.shape[0]
    return pl.pallas_call(
        _peer_ffn_kernel, grid=(t // tm, ne // PEER_STEP),
        in_specs=[pl.BlockSpec((tm, d), lambda i, j: (i, 0)),
                  pl.BlockSpec((PEER_STEP, d), lambda i, j: (j, 0)),
                  pl.BlockSpec((PEER_STEP, d), lambda i, j: (j, 0)),
                  pl.BlockSpec((2, tm, PEER_KEYS), lambda i, j: (j, i, 0))],
        out_specs=pl.BlockSpec((tm, d), lambda i, j: (i, 0)),
        out_shape=jax.ShapeDtypeStruct((t, d), F32),
        compiler_params=_cparams(2), name="peer_ffn")(xb, u, v, gmat)


def _ple_kernel(x_ref, f_ref, p_ref, g_ref, b_ref, wg_ref, bg_ref, wp_ref, o_ref):
    x2 = _layer_norm(DN_ALPHA * x_ref[...] + f_ref[...], g_ref[...], b_ref[...])
    gate = _sigmoid(_dot(x2, wg_ref[...]) + bg_ref[...])
    o_ref[...] = x2 + gate * _dot(p_ref[...], wp_ref[...])


def _ple(x, ffn, p, ln_g, ln_b, w_gate, b_gate, w_proj, *, tm):
    t, d = x.shape
    pd = p.shape[1]
    tok = pl.BlockSpec((tm, d), lambda i: (i, 0))
    vec = pl.BlockSpec((1, d), lambda i: (0, 0))
    return pl.pallas_call(
        _ple_kernel, grid=(t // tm,),
        in_specs=[tok, tok, pl.BlockSpec((tm, pd), lambda i: (i, 0)), vec, vec,
                  pl.BlockSpec((d, d), lambda i: (0, 0)), vec, pl.BlockSpec((pd, d), lambda i: (0, 0))],
        out_specs=tok, out_shape=jax.ShapeDtypeStruct((t, d), F32),
        compiler_params=_cparams(1), name="ple")(
            x, ffn, p, ln_g.reshape(1, d), ln_b.reshape(1, d), w_gate.astype(BF16), b_gate.reshape(1, d),
            w_proj.astype(BF16))


def _peer_block(xn, xn_b, w_q, subkeys, u_tab, v_tab):
    t = xn.shape[0]
    q = _matmul(xn_b, w_q.astype(BF16), tm=512, tn=512)
    ids, gates = _peer_topk(q, subkeys, tm=256)
    to_tok = lambda a: a.reshape(PEER_HEADS * PEER_TOPK, t).T
    ids = to_tok(ids)
    i1 = jnp.floor(ids * (1.0 / PEER_KEYS))
    i2 = ids - PEER_KEYS * i1
    gmat = _peer_gates(i1, i2, to_tok(gates))
    del xn
    return _peer_ffn(xn_b, u_tab.astype(BF16), v_tab.astype(BF16), gmat, tm=1024)


def _even_in_weight(w_in):
    w = RWKV_WIDTH
    rk = 3 * w
    lora = lambda c0, n: jnp.pad(w_in[:, c0:c0 + n], ((0, 0), (0, LANES - n)))
    g0 = rk + 256
    cols = [w_in[:, g0:g0 + 3 * GDN_WIDTH],
            w_in[:, 0:rk],
            w_in[:, g0 + 3 * GDN_WIDTH:g0 + 4 * GDN_WIDTH],
            lora(rk, 96), lora(rk + 96, 96), lora(rk + 192, 64),
            lora(g0 + 4 * GDN_WIDTH, 2 * GDN_HEADS)]
    return jnp.concatenate(cols, axis=1).astype(BF16)


def kernel(x, p, ev_w_in, ev_shift_mu, ev_rwkv_w0, ev_rwkv_w_up, ev_rwkv_a0, ev_rwkv_a_up, ev_rwkv_g_up, ev_rwkv_k_k, ev_rwkv_k_a, ev_rwkv_r_k, ev_rwkv_ln_g, ev_rwkv_ln_b, ev_gdn_conv, ev_gdn_a_log, ev_gdn_dt_bias, ev_gdn_norm_g, ev_w_out, od_w_in, od_conv_w, od_conv_b, od_r_w, od_r_b, od_i_w, od_i_b, od_lru_L, od_w_out, ln_mix_g, ln_mix_b, peer_w_q, peer_subkeys, peer_u, peer_v, ln_ffn_g, ln_ffn_b, ple_w_proj, ple_w_gate, ple_b_gate):
    batch, seq, d = x.shape
    t = batch * seq
    xt = x.reshape(t, d)
    ev = dict(ev_shift_mu=ev_shift_mu, ev_rwkv_w0=ev_rwkv_w0, ev_rwkv_w_up=ev_rwkv_w_up, ev_rwkv_a0=ev_rwkv_a0,
              ev_rwkv_a_up=ev_rwkv_a_up, ev_rwkv_g_up=ev_rwkv_g_up, ev_rwkv_k_k=ev_rwkv_k_k,
              ev_rwkv_k_a=ev_rwkv_k_a, ev_rwkv_r_k=ev_rwkv_r_k, ev_rwkv_ln_g=ev_rwkv_ln_g,
              ev_rwkv_ln_b=ev_rwkv_ln_b, ev_gdn_conv=ev_gdn_conv, ev_gdn_a_log=ev_gdn_a_log,
              ev_gdn_dt_bias=ev_gdn_dt_bias, ev_gdn_norm_g=ev_gdn_norm_g)
    for layer in range(p.shape[0]):
        j = layer // 2
        if layer % 2 == 0:
            proj = _matmul(xt, _even_in_weight(ev_w_in[j]), tm=512, tn=512)
            y_a, y_b = _even_mixers(proj, batch, seq, {k: v[j] for k, v in ev.items()})
            w_out = ev_w_out[j].astype(BF16)
            xn, xn_b = _proj_ln([y_a, y_b], [w_out[:RWKV_WIDTH], w_out[RWKV_WIDTH:]], xt,
                                ln_mix_g[layer], ln_mix_b[layer], tm=256)
        else:
            proj = _matmul(xt, od_w_in[j].astype(BF16), tm=512, tn=512)
            y = _lru(proj, od_conv_w[j], od_conv_b[j], od_r_w[j], od_r_b[j], od_i_w[j], od_i_b[j],
                     od_lru_L[j], batch=batch, seq=seq, tm=512)
            xn, xn_b = _proj_ln([y], [od_w_out[j].astype(BF16)], xt, ln_mix_g[layer], ln_mix_b[layer], tm=256)
        ffn = _peer_block(xn, xn_b, peer_w_q[layer], peer_subkeys[layer], peer_u[layer], peer_v[layer])
        xt = _ple(xn, ffn, p[layer].reshape(t, -1), ln_ffn_g[layer], ln_ffn_b[layer], ple_w_gate[layer],
                  ple_b_gate[layer], ple_w_proj[layer], tm=256)
    return xt.reshape(batch, seq, d)
```

```python
import functools

import jax
import jax.numpy as jnp
from jax import lax
from jax.experimental import pallas as pl
from jax.experimental.pallas import tpu as pltpu

F32 = jnp.float32
BF16 = jnp.bfloat16

D_MODEL = 2048
DEPTH = 2
DN_ALPHA = (2.0 * DEPTH) ** 0.25
LN_EPS = 1e-5

RWKV_WIDTH = 1024
RWKV_HEAD = 64
RWKV_GN_EPS = 64e-5
RWKV_CHUNK = 64
GDN_WIDTH = 1024
GDN_HEAD = 128
GDN_HEADS = 8
GDN_CHUNK = 128
GDN_NORM_EPS = 1e-6
LRU_BLOCKS = 8
LRU_BLOCK = 256
LRU_C = 8.0
PEER_HEADS = 8
PEER_KEYS = 128
PEER_TOPK = 16

LANES = 128
SUBLANES = 8
VMEM_LIMIT = 56 * 1024 * 1024

EV_QKV0 = 0
EV_RKV0 = 3072
EV_Z0 = 6144
EV_LORA0 = 7168
EV_BA0 = 7552
EV_COLS = 7680


def _cparams(n_axes):
    return pltpu.CompilerParams(dimension_semantics=("arbitrary",) * n_axes,
                                vmem_limit_bytes=VMEM_LIMIT)


def _dot(a, b):
    a, b = a.astype(BF16), b.astype(BF16)
    if a.ndim == 3:
        return lax.dot_general(a, b, (((2,), (1,)), ((0,), (0,))), preferred_element_type=F32)
    return jnp.dot(a, b, preferred_element_type=F32)


def _dot_nt(a, b):
    a, b = a.astype(BF16), b.astype(BF16)
    if a.ndim == 3:
        return lax.dot_general(a, b, (((2,), (2,)), ((0,), (0,))), preferred_element_type=F32)
    return lax.dot_general(a, b, (((1,), (1,)), ((), ())), preferred_element_type=F32)


def _dot_tn(a, b):
    return _dot(jnp.swapaxes(a, -1, -2), b)


def _split(x):
    hi = x.astype(BF16)
    lo = (x - hi.astype(F32)).astype(BF16)
    return hi, lo


def _dot_lhs2(a, b):
    hi, lo = _split(a)
    return _dot(hi, b) + _dot(lo, b)


def _dot_rhs2(a, b):
    hi, lo = _split(b)
    return _dot(a, hi) + _dot(a, lo)


def _dot3(a, b):
    ah, al = _split(a)
    bh, bl = _split(b)
    return _dot(ah, bh) + _dot(al, bh) + _dot(ah, bl)


def _sigmoid(x):
    return 1.0 / (1.0 + jnp.exp(-x))


def _softplus(x):
    return jnp.maximum(x, 0.0) + jnp.log(1.0 + jnp.exp(-jnp.abs(x)))


def _gelu(x):
    return 0.5 * x * (1.0 + jnp.tanh(0.7978845608028654 * (x + 0.044715 * (x * x * x))))


def _iota2(shape, axis):
    return lax.broadcasted_iota(jnp.int32, shape, axis)


def _layer_norm(x, g, b):
    mu = jnp.mean(x, axis=-1, keepdims=True)
    xc = x - mu
    var = jnp.mean(xc * xc, axis=-1, keepdims=True)
    return xc * lax.rsqrt(var + LN_EPS) * g + b


def _shift_rows(x, halo, k):
    rolled = pltpu.roll(x, k, 0)
    hr = pltpu.roll(halo, k, 0)
    row = _iota2(halo.shape, 0)
    top = jnp.where(row < k, hr, rolled[0:SUBLANES])
    return jnp.concatenate([top, rolled[SUBLANES:]], axis=0)


def _tri_inv(x, n, top):
    row = _iota2((n, n), 0)
    col = _iota2((n, n), 1)
    eye = (row == col).astype(F32)
    x0 = jnp.where((row >> 3) == (col >> 3), x, 0.0)
    x2 = _dot(x0, x0)
    x4 = _dot(x2, x2)
    t = eye + x0
    t = t + _dot(t, x2)
    t = t + _dot(t, x4)
    s = 8
    while s < top:
        sh = s.bit_length() - 1
        off = ((row >> (sh + 1)) == (col >> (sh + 1))) & ((row >> sh) != (col >> sh))
        xo = jnp.where(off, x, 0.0)
        t = t + _dot(_dot(t, xo), t)
        s *= 2
    return t


def _mm_kernel(x_ref, w_ref, o_ref):
    o_ref[...] = jnp.dot(x_ref[...].astype(BF16), w_ref[...],
                         preferred_element_type=F32).astype(o_ref.dtype)


def _matmul(x, w, *, tm, tn, out_dtype=F32):
    m, k = x.shape
    n = w.shape[1]
    return pl.pallas_call(
        _mm_kernel, grid=(m // tm, n // tn),
        in_specs=[pl.BlockSpec((tm, k), lambda i, j: (i, 0)),
                  pl.BlockSpec((k, tn), lambda i, j: (0, j))],
        out_specs=pl.BlockSpec((tm, tn), lambda i, j: (i, j)),
        out_shape=jax.ShapeDtypeStruct((m, n), out_dtype),
        compiler_params=_cparams(2), name="matmul")(x, w)


def _proj_ln_kernel(*refs, n_in):
    ys = refs[:n_in]
    ws = refs[n_in:2 * n_in]
    x_ref, g_ref, b_ref, o_ref, ob_ref = refs[2 * n_in:]
    mix = _dot(ys[0][...], ws[0][...])
    for y_ref, w_ref in zip(ys[1:], ws[1:]):
        mix = mix + _dot(y_ref[...], w_ref[...])
    xn = _layer_norm(DN_ALPHA * x_ref[...] + mix, g_ref[...], b_ref[...])
    o_ref[...] = xn
    ob_ref[...] = xn.astype(BF16)


def _proj_ln(ys, ws, x, g, b, *, tm):
    t, d = x.shape
    n_in = len(ys)
    in_specs = ([pl.BlockSpec((tm, y.shape[1]), lambda i: (i, 0)) for y in ys]
                + [pl.BlockSpec(w.shape, lambda i: (0, 0)) for w in ws]
                + [pl.BlockSpec((tm, d), lambda i: (i, 0)),
                   pl.BlockSpec((1, d), lambda i: (0, 0)),
                   pl.BlockSpec((1, d), lambda i: (0, 0))])
    return pl.pallas_call(
        functools.partial(_proj_ln_kernel, n_in=n_in), grid=(t // tm,),
        in_specs=in_specs,
        out_specs=[pl.BlockSpec((tm, d), lambda i: (i, 0)), pl.BlockSpec((tm, d), lambda i: (i, 0))],
        out_shape=[jax.ShapeDtypeStruct((t, d), F32), jax.ShapeDtypeStruct((t, d), BF16)],
        compiler_params=_cparams(1), name="proj_ln")(*ys, *ws, x, g.reshape(1, d), b.reshape(1, d))


def _rwkv_pre_kernel(p_ref, ph_ref, l_ref, lh_ref, mu_ref, mul_ref, w0_ref, wup_ref, a0_ref, aup_ref,
                     gup_ref, kk_ref, ka_ref, rk_ref, bd_ref,
                     r_o, lw_o, k_o, v_o, a_o, b_o, bonus_o, g_o, *, tiles_per_seq):
    first = (pl.program_id(0) % tiles_per_seq) == 0
    w = RWKV_WIDTH

    def mixed(ref, href, m_ref, c0, c1):
        x = ref[:, c0:c1]
        hrow = jnp.where(first, 0.0, href[SUBLANES - 1:SUBLANES, c0:c1])
        prev = pltpu.roll(x, 1, 0)
        prev = jnp.where(_iota2(x.shape, 0) == 0, hrow, prev)
        return x + (prev - x) * m_ref[:, c0:c1]

    def segsum(x):
        bd = bd_ref[...]
        return jnp.concatenate(
            [_dot_lhs2(x[:, LANES * p:LANES * (p + 1)], bd) for p in range(w // LANES)], axis=1)

    r = mixed(p_ref, ph_ref, mu_ref, 0, w)
    k = mixed(p_ref, ph_ref, mu_ref, w, 2 * w)
    v = mixed(p_ref, ph_ref, mu_ref, 2 * w, 3 * w)
    wd = mixed(l_ref, lh_ref, mul_ref, 0, LANES)
    ad = mixed(l_ref, lh_ref, mul_ref, LANES, 2 * LANES)
    gd = mixed(l_ref, lh_ref, mul_ref, 2 * LANES, 3 * LANES)

    w_log = -_softplus(-(w0_ref[...] + _dot3(jnp.tanh(wd), wup_ref[...]))) - 0.5
    lw_o[...] = -jnp.exp(w_log)
    a = _sigmoid(a0_ref[...] + _dot3(ad, aup_ref[...]))
    g_o[...] = _dot3(_sigmoid(gd), gup_ref[...])
    kkr = k * kk_ref[...]
    kk = kkr * lax.rsqrt(segsum(kkr * kkr) + 1e-6)
    k2 = k * (1.0 + (a - 1.0) * ka_ref[...])
    bonus_o[...] = segsum(r * k2 * rk_ref[...]) * v
    r_o[...] = r
    k_o[...] = k2
    v_o[...] = v
    a_o[...] = -kk
    b_o[...] = kk * a


def _rwkv_pre(proj, seq, mu_rkv, mu_lora, w0, w_up, a0, a_up, g_up, k_k, k_a, r_k, bd64, *, tm):
    t = proj.shape[0]
    w = RWKV_WIDTH
    nh = tm // SUBLANES
    row = lambda a: a.reshape(1, -1)
    full = lambda a: pl.BlockSpec(a.shape, lambda i: (0, 0))
    args = [row(mu_rkv), row(mu_lora), row(w0), w_up, row(a0), a_up, g_up, row(k_k), row(k_a), row(r_k), bd64]
    in_specs = [pl.BlockSpec((tm, 3 * w), lambda i: (i, EV_RKV0 // (3 * w))),
                pl.BlockSpec((SUBLANES, 3 * w), lambda i: (jnp.maximum(i * nh - 1, 0), EV_RKV0 // (3 * w))),
                pl.BlockSpec((tm, 512), lambda i: (i, EV_LORA0 // 512)),
                pl.BlockSpec((SUBLANES, 512), lambda i: (jnp.maximum(i * nh - 1, 0), EV_LORA0 // 512))]
    in_specs += [full(a) for a in args]
    out_spec = pl.BlockSpec((tm, w), lambda i: (i, 0))
    return pl.pallas_call(
        functools.partial(_rwkv_pre_kernel, tiles_per_seq=seq // tm), grid=(t // tm,),
        in_specs=in_specs, out_specs=[out_spec] * 8,
        out_shape=[jax.ShapeDtypeStruct((t, w), F32)] * 8,
        compiler_params=_cparams(1), name="rwkv_pre")(proj, proj, proj, proj, *args)


def _rwkv_chunk_kernel(r_ref, lw_ref, k_ref, v_ref, a_ref, b_ref, bonus_ref, g_ref, lng_ref, lnb_ref,
                       bd_ref, o_ref, z_ref, *, n_chunks):
    L = RWKV_CHUNK

    @pl.when(pl.program_id(2) == 0)
    def _():
        z_ref[...] = jnp.zeros_like(z_ref)

    nc = n_chunks
    lane = _iota2((nc, L, LANES), 2)
    head0 = lane < RWKV_HEAD
    row = _iota2((2 * L, 2 * L), 0)
    col = _iota2((2 * L, 2 * L), 1)
    strict = row > col
    incl = row >= col
    eye = row == col
    tril = jnp.broadcast_to((_iota2((L, L), 0) >= _iota2((L, L), 1)).astype(BF16), (nc, L, L))
    bd = bd_ref[...]

    def stack2(x):
        return jnp.concatenate([jnp.where(head0, x, 0.0), jnp.where(head0, 0.0, x)], axis=1)

    r, lw, k, v, a, b = (ref[...].reshape(nc, L, LANES) for ref in (r_ref, lw_ref, k_ref, v_ref, a_ref, b_ref))
    cl = _dot_rhs2(tril, lw)
    cl_last = cl[:, L - 1:L, :]
    e_in = jnp.exp(cl)
    e_out = jnp.exp(-cl)
    e_end = jnp.exp(cl_last - cl)
    rt = stack2(r * e_in)
    at = stack2(a * jnp.exp(cl - lw))
    kt = stack2(k * e_out)
    bt = stack2(b * e_out)
    kc = stack2(k * e_end)
    bc = stack2(b * e_end)
    vs = stack2(v)
    aa = _dot_nt(jnp.concatenate([at, rt], axis=1), jnp.concatenate([bt, kt], axis=1))
    a_ab = jnp.where(strict, aa[:, 0:2 * L, 0:2 * L], 0.0)
    a_ak = jnp.where(strict, aa[:, 0:2 * L, 2 * L:4 * L], 0.0)
    a_rb = jnp.where(incl, aa[:, 2 * L:4 * L, 0:2 * L], 0.0)
    a_rk = jnp.where(incl, aa[:, 2 * L:4 * L, 2 * L:4 * L], 0.0)
    tm = _tri_inv(a_ab, 2 * L, L)
    a_hat = _dot(tm, at)
    u_v = _dot(tm, _dot(a_ak, vs))
    r_hat = rt + _dot(a_rb, a_hat)
    y_v = _dot(a_rb, u_v) + _dot(a_rk, vs)
    m = jnp.where(eye, jnp.exp(cl_last), 0.0) + _dot_tn(bc, a_hat)
    n = _dot_tn(bc, u_v) + _dot_tn(kc, vs)

    z = z_ref[...]
    ys = []
    for c in range(nc):
        yc2 = _dot(r_hat[c], z) + y_v[c]
        z = _dot(m[c], z) + n[c]
        ys.append(yc2[0:L] + yc2[L:2 * L])
    z_ref[...] = z
    y = jnp.concatenate(ys, axis=0)

    mean = _dot_lhs2(y, bd) * (1.0 / RWKV_HEAD)
    yc = y - mean
    var = _dot_lhs2(yc * yc, bd) * (1.0 / RWKV_HEAD)
    yn = yc * lax.rsqrt(var + RWKV_GN_EPS) * lng_ref[...] + lnb_ref[...]
    o_ref[...] = ((yn + bonus_ref[...]) * g_ref[...]).astype(o_ref.dtype)


def _rwkv_chunks(r, lw, k, v, a, b, bonus, g, ln_g, ln_b, bd64, *, batch, seq, tb):
    t, w = r.shape
    npairs = w // LANES
    nblk = seq // tb
    tok = pl.BlockSpec((tb, LANES), lambda bi, p, j: (bi * nblk + j, p))
    par = pl.BlockSpec((1, LANES), lambda bi, p, j: (0, p))
    return pl.pallas_call(
        functools.partial(_rwkv_chunk_kernel, n_chunks=tb // RWKV_CHUNK),
        grid=(batch, npairs, nblk),
        in_specs=[tok] * 8 + [par, par, pl.BlockSpec((LANES, LANES), lambda bi, p, j: (0, 0))],
        out_specs=tok,
        out_shape=jax.ShapeDtypeStruct((t, w), BF16),
        scratch_shapes=[pltpu.VMEM((LANES, LANES), F32)],
        compiler_params=_cparams(3), name="rwkv_chunks")(
            r, lw, k, v, a, b, bonus, g, ln_g.reshape(1, w), ln_b.reshape(1, w), bd64)


def _gdn_pre_kernel(p_ref, ph_ref, ba_ref, cw_ref, alog_ref, dtb_ref,
                    q_o, k_o, kb_o, vb_o, gb_o, *, tiles_per_seq):
    first = (pl.program_id(0) % tiles_per_seq) == 0
    w = GDN_WIDTH
    tm = p_ref.shape[0]
    ones = jnp.ones((LANES, LANES), BF16)

    def conv_silu(c0, c1):
        x = p_ref[:, c0:c1]
        halo = jnp.where(first, 0.0, ph_ref[:, c0:c1])
        out = x * cw_ref[3:4, c0:c1]
        for kshift in (1, 2, 3):
            out = out + _shift_rows(x, halo, kshift) * cw_ref[3 - kshift:4 - kshift, c0:c1]
        return out * _sigmoid(out)

    ba = ba_ref[...]
    beta = _sigmoid(ba)
    glog = -jnp.exp(alog_ref[...]) * _softplus(ba + dtb_ref[...])
    srow = _iota2((LANES, LANES), 0)
    for h in range(GDN_HEADS):
        sl = slice(GDN_HEAD * h, GDN_HEAD * (h + 1))
        beta_b = _dot_lhs2(beta, (srow == h).astype(BF16))
        gb_o[:, sl] = _dot_lhs2(glog, (srow == GDN_HEADS + h).astype(BF16))
        q = conv_silu(GDN_HEAD * h, GDN_HEAD * (h + 1))
        k = conv_silu(w + GDN_HEAD * h, w + GDN_HEAD * (h + 1))
        v = conv_silu(2 * w + GDN_HEAD * h, 2 * w + GDN_HEAD * (h + 1))
        qn = q * lax.rsqrt(_dot_lhs2(q * q, ones) + 1e-6) * (GDN_HEAD ** -0.5)
        kn = k * lax.rsqrt(_dot_lhs2(k * k, ones) + 1e-6)
        q_o[:, sl] = qn.astype(BF16)
        k_o[:, sl] = kn.astype(BF16)
        kb_o[:, sl] = (kn * beta_b).astype(BF16)
        vb_o[:, sl] = (v * beta_b).astype(BF16)
    del tm


def _gdn_pre(proj, seq, conv_w, a_log, dt_bias, *, tm):
    t = proj.shape[0]
    w = GDN_WIDTH
    nh = tm // SUBLANES
    pad = lambda a: jnp.zeros((1, LANES), F32).at[0, GDN_HEADS:2 * GDN_HEADS].set(a)
    full = lambda a: pl.BlockSpec(a.shape, lambda i: (0, 0))
    args = [conv_w, pad(a_log), pad(dt_bias)]
    out_spec = pl.BlockSpec((tm, w), lambda i: (i, 0))
    return pl.pallas_call(
        functools.partial(_gdn_pre_kernel, tiles_per_seq=seq // tm), grid=(t // tm,),
        in_specs=[pl.BlockSpec((tm, 3 * w), lambda i: (i, EV_QKV0 // (3 * w))),
                  pl.BlockSpec((SUBLANES, 3 * w), lambda i: (jnp.maximum(i * nh - 1, 0), EV_QKV0 // (3 * w))),
                  pl.BlockSpec((tm, LANES), lambda i: (i, EV_BA0 // LANES))] + [full(a) for a in args],
        out_specs=[out_spec] * 5,
        out_shape=[jax.ShapeDtypeStruct((t, w), BF16)] * 4 + [jax.ShapeDtypeStruct((t, w), F32)],
        compiler_params=_cparams(1), name="gdn_pre")(proj, proj, proj, *args)


def _gdn_chunk_kernel(q_ref, k_ref, kb_ref, vb_ref, gb_ref, z_ref, ng_ref, o_ref, s_ref, *, n_chunks):
    C = GDN_CHUNK

    @pl.when(pl.program_id(2) == 0)
    def _():
        s_ref[...] = jnp.zeros_like(s_ref)

    row = _iota2((C, C), 0)
    col = _iota2((C, C), 1)
    strict = row > col
    incl = row >= col
    eye = row == col
    nc = n_chunks
    tril = jnp.broadcast_to(incl.astype(BF16), (nc, C, C))

    q, k, kb, vb = (ref[...].astype(F32).reshape(nc, C, GDN_HEAD) for ref in (q_ref, k_ref, kb_ref, vb_ref))
    gc = _dot_rhs2(tril, gb_ref[...].reshape(nc, C, GDN_HEAD))
    gct = jnp.swapaxes(gc, 1, 2)
    dm = jnp.exp(jnp.where(incl, gc - gct, -1e30))
    egc = jnp.exp(gc)
    g_last = gc[:, C - 1:C, :]
    kk = _dot_nt(kb, k)
    qk = _dot_nt(q, k)
    m_mat = jnp.where(strict, kk * dm, 0.0)
    attn = qk * dm
    tinv = _tri_inv(-m_mat, C, C)
    u = _dot(tinv, vb)
    w = _dot(tinv, kb * egc)
    q_hat = q * egc - _dot(attn, w)
    o_v = _dot(attn, u)
    kd = k * jnp.exp(g_last - gc)
    mm = jnp.where(eye, jnp.exp(g_last), 0.0) - _dot_tn(kd, w)
    nn = _dot_tn(kd, u)

    s = s_ref[...]
    os_ = []
    for c in range(nc):
        os_.append(_dot(q_hat[c], s) + o_v[c])
        s = _dot(mm[c], s) + nn[c]
    s_ref[...] = s
    o = jnp.concatenate(os_, axis=0)
    o = o * lax.rsqrt(jnp.mean(o * o, axis=-1, keepdims=True) + GDN_NORM_EPS) * ng_ref[...]
    zz = z_ref[...]
    o_ref[...] = (o * (zz * _sigmoid(zz))).astype(o_ref.dtype)


def _gdn_chunks(q, k, kb, vb, gb, proj, norm_g, *, batch, seq, tb):
    t, w = q.shape
    nblk = seq // tb
    tok = pl.BlockSpec((tb, GDN_HEAD), lambda bi, h, j: (bi * nblk + j, h))
    zspec = pl.BlockSpec((tb, GDN_HEAD), lambda bi, h, j: (bi * nblk + j, EV_Z0 // GDN_HEAD + h))
    return pl.pallas_call(
        functools.partial(_gdn_chunk_kernel, n_chunks=tb // GDN_CHUNK),
        grid=(batch, GDN_HEADS, nblk),
        in_specs=[tok] * 5 + [zspec, pl.BlockSpec((1, GDN_HEAD), lambda bi, h, j: (0, 0))],
        out_specs=tok,
        out_shape=jax.ShapeDtypeStruct((t, w), BF16),
        scratch_shapes=[pltpu.VMEM((GDN_HEAD, GDN_HEAD), F32)],
        compiler_params=_cparams(3), name="gdn_chunks")(q, k, kb, vb, gb, proj, norm_g.reshape(1, GDN_HEAD))


def _head_block_ones():
    i = jnp.arange(LANES)
    return ((i[:, None] // RWKV_HEAD) == (i[None, :] // RWKV_HEAD)).astype(BF16)


def _pad_rows(a, rows):
    return jnp.zeros((rows,) + a.shape[1:], a.dtype).at[:a.shape[0]].set(a)


def _even_mixers(proj, batch, seq, prm):
    bd64 = _head_block_ones()
    mu = prm['ev_shift_mu']
    w = RWKV_WIDTH
    mu_lora = jnp.zeros((512,), F32)
    mu_lora = mu_lora.at[0:96].set(mu[3 * w:3 * w + 96]).at[128:224].set(mu[3 * w + 96:3 * w + 192])
    mu_lora = mu_lora.at[256:320].set(mu[3 * w + 192:3 * w + 256])
    pre = _rwkv_pre(proj, seq, mu[:3 * w], mu_lora, prm['ev_rwkv_w0'], _pad_rows(prm['ev_rwkv_w_up'], LANES),
                    prm['ev_rwkv_a0'], _pad_rows(prm['ev_rwkv_a_up'], LANES),
                    _pad_rows(prm['ev_rwkv_g_up'], LANES), prm['ev_rwkv_k_k'], prm['ev_rwkv_k_a'],
                    prm['ev_rwkv_r_k'].reshape(-1), bd64, tm=256)
    y_a = _rwkv_chunks(*pre, prm['ev_rwkv_ln_g'], prm['ev_rwkv_ln_b'], bd64, batch=batch, seq=seq, tb=1024)
    gpre = _gdn_pre(proj, seq, prm['ev_gdn_conv'], prm['ev_gdn_a_log'], prm['ev_gdn_dt_bias'], tm=256)
    y_b = _gdn_chunks(*gpre, proj, prm['ev_gdn_norm_g'], batch=batch, seq=seq, tb=2048)
    return y_a, y_b


def _lru_kernel(x_ref, xh_ref, gate_ref, cw_ref, cb_ref, rw_ref, rb_ref, iw_ref, ib_ref, l_ref,
                o_ref, h_ref):
    j = pl.program_id(2)
    tm = x_ref.shape[0]

    @pl.when(j == 0)
    def _():
        h_ref[...] = jnp.zeros_like(h_ref)

    x = x_ref[...]
    halo = jnp.where(j == 0, 0.0, xh_ref[...])
    xc = x * cw_ref[3:4, :] + cb_ref[...]
    for kshift in (1, 2, 3):
        xc = xc + _shift_rows(x, halo, kshift) * cw_ref[3 - kshift:4 - kshift, :]
    r = _sigmoid(_dot(xc, rw_ref[0]) + rb_ref[...])
    i = _sigmoid(_dot(xc, iw_ref[0]) + ib_ref[...])
    log_a = -LRU_C * r * _softplus(-l_ref[...])
    a = jnp.exp(log_a)
    b = jnp.sqrt(1.0 - jnp.exp(2.0 * log_a)) * (i * xc)
    row = _iota2(a.shape, 0)
    s = 1
    while s < tm:
        keep = row >= s
        a_sh = jnp.where(keep, pltpu.roll(a, s, 0), 1.0)
        b_sh = jnp.where(keep, pltpu.roll(b, s, 0), 0.0)
        b = a * b_sh + b
        a = a * a_sh
        s *= 2
    h = b + a * h_ref[0:1, :]
    h_ref[...] = jnp.broadcast_to(h[tm - 1:tm, :], h_ref.shape)
    o_ref[...] = (h * _gelu(gate_ref[...])).astype(o_ref.dtype)


def _lru(proj, conv_w, conv_b, r_w, r_b, i_w, i_b, lru_l, *, batch, seq, tm):
    t = proj.shape[0]
    w = LRU_BLOCKS * LRU_BLOCK
    nblk = seq // tm
    nh = tm // SUBLANES
    vec = pl.BlockSpec((1, LRU_BLOCK), lambda bi, g, j: (0, g))
    mat = pl.BlockSpec((1, LRU_BLOCK, LRU_BLOCK), lambda bi, g, j: (g, 0, 0))
    row = lambda a: a.reshape(1, w)
    return pl.pallas_call(
        _lru_kernel, grid=(batch, LRU_BLOCKS, nblk),
        in_specs=[pl.BlockSpec((tm, LRU_BLOCK), lambda bi, g, j: (bi * nblk + j, LRU_BLOCKS + g)),
                  pl.BlockSpec((SUBLANES, LRU_BLOCK),
                               lambda bi, g, j: (jnp.maximum((bi * nblk + j) * nh - 1, 0), LRU_BLOCKS + g)),
                  pl.BlockSpec((tm, LRU_BLOCK), lambda bi, g, j: (bi * nblk + j, g)),
                  pl.BlockSpec((4, LRU_BLOCK), lambda bi, g, j: (0, g)), vec, mat, vec, mat, vec, vec],
        out_specs=pl.BlockSpec((tm, LRU_BLOCK), lambda bi, g, j: (bi * nblk + j, g)),
        out_shape=jax.ShapeDtypeStruct((t, w), BF16),
        scratch_shapes=[pltpu.VMEM((SUBLANES, LRU_BLOCK), F32)],
        compiler_params=_cparams(3), name="rglru")(
            proj, proj, proj, conv_w, row(conv_b), r_w.astype(BF16), row(r_b), i_w.astype(BF16), row(i_b),
            row(lru_l))


def _oddeven_merge_sort_network(n):
    pairs = []

    def merge(lo, hi, r):
        step = 2 * r
        if step < hi - lo:
            merge(lo, hi, step)
            merge(lo + r, hi, step)
            pairs.extend((i, i + r) for i in range(lo + r, hi - r, step))
        else:
            pairs.append((lo, lo + r))

    def sort(lo, hi):
        if hi - lo >= 1:
            mid = lo + (hi - lo) // 2
            sort(lo, mid)
            sort(mid + 1, hi)
            merge(lo, hi, 1)

    sort(0, n - 1)
    return tuple(pairs)


_SORT16 = _oddeven_merge_sort_network(PEER_KEYS // SUBLANES)


def _peer_topk_kernel(q_ref, sk_ref, id_o, gate_o, v_ref, i_ref, s_ref, c_ref):
    tm = q_ref.shape[0]
    neg = -jnp.inf
    ngrp = PEER_KEYS // SUBLANES
    sub = _iota2((SUBLANES, tm), 0).astype(F32)

    for half in range(2):
        qh = q_ref[:, PEER_KEYS * half:PEER_KEYS * (half + 1)]
        ah, al = _split(sk_ref[0, half])
        bh, bl = _split(qh)
        nt = lambda a, b: lax.dot_general(a, b, (((1,), (1,)), ((), ())), preferred_element_type=F32)
        scores = nt(ah, bh) + nt(al, bh) + nt(ah, bl)

        rows = [scores[SUBLANES * a:SUBLANES * (a + 1), :] for a in range(ngrp)]
        keys = [sub + float(SUBLANES * a) for a in range(ngrp)]
        for i, j in _SORT16:
            swap = rows[j] > rows[i]
            rows[i], rows[j] = jnp.maximum(rows[i], rows[j]), jnp.minimum(rows[i], rows[j])
            keys[i], keys[j] = jnp.where(swap, keys[j], keys[i]), jnp.where(swap, keys[i], keys[j])
        for kk in range(PEER_TOPK):
            m = jnp.max(rows[0], axis=0, keepdims=True)
            idx = jnp.min(jnp.where(rows[0] == m, keys[0], 1e9), axis=0, keepdims=True)
            v_ref[half, kk:kk + 1, :] = m
            i_ref[half, kk:kk + 1, :] = idx
            win = keys[0] == idx
            for a in range(PEER_TOPK - 1 - kk):
                rows[a] = jnp.where(win, rows[a + 1], rows[a])
                keys[a] = jnp.where(win, keys[a + 1], keys[a])

    v1, v2 = v_ref[0], v_ref[1]
    i1, i2 = i_ref[0], i_ref[1]
    cands = [v1[0:1] + v2]
    ids = [i1[0:1] * PEER_KEYS + i2]
    for a in range(1, 8):
        cands.append(v1[a:a + 1] + v2[0:8])
        ids.append(i1[a:a + 1] * PEER_KEYS + i2[0:8])
    cands.append(v1[8:16] + v2[0:1])
    ids.append(i1[8:16] * PEER_KEYS + i2[0:1])
    cand = jnp.concatenate(cands, axis=0)
    c_ref[...] = jnp.concatenate(ids, axis=0)
    piota = _iota2(cand.shape, 0).astype(F32)

    def pick2(kk, c):
        m = jnp.max(c, axis=0, keepdims=True)
        pos = jnp.min(jnp.where(c == m, piota, 1e9), axis=0, keepdims=True)
        hit = piota == pos
        s_ref[pl.ds(kk, 1), :] = m
        id_o[0, pl.ds(kk, 1), :] = jnp.max(jnp.where(hit, c_ref[...], -1.0), axis=0, keepdims=True)
        return jnp.where(hit, neg, c)

    lax.fori_loop(0, PEER_TOPK, pick2, cand)
    top = s_ref[...]
    e = jnp.exp(top - top[0:1])
    gate_o[0] = e / jnp.sum(e, axis=0, keepdims=True)


def _peer_topk(q, subkeys, *, tm):
    t = q.shape[0]
    out = pl.BlockSpec((1, PEER_TOPK, tm), lambda i, h: (h, 0, i))
    return pl.pallas_call(
        _peer_topk_kernel, grid=(t // tm, PEER_HEADS),
        in_specs=[pl.BlockSpec((tm, 2 * PEER_KEYS), lambda i, h: (i, h)),
                  pl.BlockSpec((1, 2, PEER_KEYS, PEER_KEYS), lambda i, h: (h, 0, 0, 0))],
        out_specs=[out, out],
        out_shape=[jax.ShapeDtypeStruct((PEER_HEADS, PEER_TOPK, t), F32)] * 2,
        scratch_shapes=[pltpu.VMEM((2, PEER_TOPK, tm), F32), pltpu.VMEM((2, PEER_TOPK, tm), F32),
                        pltpu.VMEM((PEER_TOPK, tm), F32), pltpu.VMEM((80, tm), F32)],
        compiler_params=_cparams(2), name="peer_topk")(q, subkeys)


GATE_TILE = 64
GATE_STRIDE = GATE_TILE + SUBLANES


def _peer_gates_kernel(i1_ref, i2_ref, g_ref, o_ref, s_ref):
    ciota = _iota2((PEER_KEYS, LANES), 0).astype(F32)

    def body(t, carry):
        i1 = i1_ref[pl.ds(t, 1), :]
        i2 = i2_ref[pl.ds(t, 1), :]
        g = g_ref[pl.ds(t, 1), :]
        a = jnp.where(ciota == i1, g, 0.0)
        bt = jnp.where(ciota == i2, 1.0, 0.0)
        s_ref[pl.ds(t, PEER_KEYS, stride=GATE_STRIDE), :] = _dot_nt(a, bt)
        return carry

    lax.fori_loop(0, GATE_TILE, body, 0, unroll=8)
    for c in range(PEER_KEYS):
        o_ref[c] = s_ref[GATE_STRIDE * c:GATE_STRIDE * c + GATE_TILE, :].astype(o_ref.dtype)


def _peer_gates(i1, i2, gates):
    t = i1.shape[0]
    tok = pl.BlockSpec((GATE_TILE, LANES), lambda i: (i, 0))
    return pl.pallas_call(
        _peer_gates_kernel, grid=(t // GATE_TILE,),
        in_specs=[tok, tok, tok],
        out_specs=pl.BlockSpec((PEER_KEYS, GATE_TILE, PEER_KEYS), lambda i: (0, i, 0)),
        out_shape=jax.ShapeDtypeStruct((PEER_KEYS, t, PEER_KEYS), BF16),
        scratch_shapes=[pltpu.VMEM((PEER_KEYS * GATE_STRIDE, PEER_KEYS), F32)],
        compiler_params=_cparams(1), name="peer_gates")(i1, i2, gates)


PEER_STEP = 4 * PEER_KEYS


def _peer_ffn_kernel(x_ref, u_ref, v_ref, g_ref, o_ref):
    @pl.when(pl.program_id(1) == 0)
    def _():
        o_ref[...] = jnp.zeros_like(o_ref)

    h = lax.dot_general(x_ref[...], u_ref[...], (((1,), (1,)), ((), ())), preferred_element_type=F32)
    gate = jnp.concatenate([g_ref[c] for c in range(PEER_STEP // PEER_KEYS)], axis=1).astype(F32)
    wgt = (_gelu(h) * gate).astype(BF16)
    o_ref[...] += jnp.dot(wgt, v_ref[...], preferred_element_type=F32)


def _peer_ffn(xb, u, v, gmat, *, tm):
    t, d = xb.shape
    ne = u.shape[0]
    return pl.pallas_call(
        _peer_ffn_kernel, grid=(t // tm, ne // PEER_STEP),
        in_specs=[pl.BlockSpec((tm, d), lambda i, j: (i, 0)),
                  pl.BlockSpec((PEER_STEP, d), lambda i, j: (j, 0)),
                  pl.BlockSpec((PEER_STEP, d), lambda i, j: (j, 0)),
                  pl.BlockSpec((PEER_STEP // PEER_KEYS, tm, PEER_KEYS), lambda i, j: (j, i, 0))],
        out_specs=pl.BlockSpec((tm, d), lambda i, j: (i, 0)),
        out_shape=jax.ShapeDtypeStruct((t, d), F32),
        compiler_params=_cparams(2), name="peer_ffn")(xb, u, v, gmat)


def _ple_kernel(x_ref, f_ref, p_ref, g_ref, b_ref, wg_ref, bg_ref, wp_ref, o_ref):
    x2 = _layer_norm(DN_ALPHA * x_ref[...] + f_ref[...], g_ref[...], b_ref[...])
    gate = _sigmoid(_dot(x2, wg_ref[...]) + bg_ref[...])
    o_ref[...] = x2 + gate * _dot(p_ref[...], wp_ref[...])


def _ple(x, ffn, p, ln_g, ln_b, w_gate, b_gate, w_proj, *, tm):
    t, d = x.shape
    pd = p.shape[1]
    tok = pl.BlockSpec((tm, d), lambda i: (i, 0))
    vec = pl.BlockSpec((1, d), lambda i: (0, 0))
    return pl.pallas_call(
        _ple_kernel, grid=(t // tm,),
        in_specs=[tok, tok, pl.BlockSpec((tm, pd), lambda i: (i, 0)), vec, vec,
                  pl.BlockSpec((d, d), lambda i: (0, 0)), vec, pl.BlockSpec((pd, d), lambda i: (0, 0))],
        out_specs=tok, out_shape=jax.ShapeDtypeStruct((t, d), F32),
        compiler_params=_cparams(1), name="ple")(
            x, ffn, p, ln_g.reshape(1, d), ln_b.reshape(1, d), w_gate.astype(BF16), b_gate.reshape(1, d),
            w_proj.astype(BF16))


def _peer_block(xn, xn_b, w_q, subkeys, u_tab, v_tab):
    t = xn.shape[0]
    q = _matmul(xn_b, w_q.astype(BF16), tm=1024, tn=512)
    ids, gates = _peer_topk(q, subkeys, tm=256)
    to_tok = lambda a: a.reshape(PEER_HEADS * PEER_TOPK, t).T
    ids = to_tok(ids)
    i1 = jnp.floor(ids * (1.0 / PEER_KEYS))
    i2 = ids - PEER_KEYS * i1
    gmat = _peer_gates(i1, i2, to_tok(gates))
    del xn
    return _peer_ffn(xn_b, u_tab.astype(BF16), v_tab.astype(BF16), gmat, tm=1024)


def _even_in_weight(w_in):
    w = RWKV_WIDTH
    rk = 3 * w
    lora = lambda c0, n: jnp.pad(w_in[:, c0:c0 + n], ((0, 0), (0, LANES - n)))
    g0 = rk + 256
    cols = [w_in[:, g0:g0 + 3 * GDN_WIDTH],
            w_in[:, 0:rk],
            w_in[:, g0 + 3 * GDN_WIDTH:g0 + 4 * GDN_WIDTH],
            lora(rk, 96), lora(rk + 96, 96), lora(rk + 192, 64),
            lora(g0 + 4 * GDN_WIDTH, 2 * GDN_HEADS)]
    return jnp.concatenate(cols, axis=1).astype(BF16)


def kernel(x, p, ev_w_in, ev_shift_mu, ev_rwkv_w0, ev_rwkv_w_up, ev_rwkv_a0, ev_rwkv_a_up, ev_rwkv_g_up, ev_rwkv_k_k, ev_rwkv_k_a, ev_rwkv_r_k, ev_rwkv_ln_g, ev_rwkv_ln_b, ev_gdn_conv, ev_gdn_a_log, ev_gdn_dt_bias, ev_gdn_norm_g, ev_w_out, od_w_in, od_conv_w, od_conv_b, od_r_w, od_r_b, od_i_w, od_i_b, od_lru_L, od_w_out, ln_mix_g, ln_mix_b, peer_w_q, peer_subkeys, peer_u, peer_v, ln_ffn_g, ln_ffn_b, ple_w_proj, ple_w_gate, ple_b_gate):
    batch, seq, d = x.shape
    t = batch * seq
    xt = x.reshape(t, d)
    ev = dict(ev_shift_mu=ev_shift_mu, ev_rwkv_w0=ev_rwkv_w0, ev_rwkv_w_up=ev_rwkv_w_up, ev_rwkv_a0=ev_rwkv_a0,
              ev_rwkv_a_up=ev_rwkv_a_up, ev_rwkv_g_up=ev_rwkv_g_up, ev_rwkv_k_k=ev_rwkv_k_k,
              ev_rwkv_k_a=ev_rwkv_k_a, ev_rwkv_r_k=ev_rwkv_r_k, ev_rwkv_ln_g=ev_rwkv_ln_g,
              ev_rwkv_ln_b=ev_rwkv_ln_b, ev_gdn_conv=ev_gdn_conv, ev_gdn_a_log=ev_gdn_a_log,
              ev_gdn_dt_bias=ev_gdn_dt_bias, ev_gdn_norm_g=ev_gdn_norm_g)
    for layer in range(p.shape[0]):
        j = layer // 2
        if layer % 2 == 0:
            proj = _matmul(xt, _even_in_weight(ev_w_in[j]), tm=1024, tn=512)
            y_a, y_b = _even_mixers(proj, batch, seq, {k: v[j] for k, v in ev.items()})
            w_out = ev_w_out[j].astype(BF16)
            xn, xn_b = _proj_ln([y_a, y_b], [w_out[:RWKV_WIDTH], w_out[RWKV_WIDTH:]], xt,
                                ln_mix_g[layer], ln_mix_b[layer], tm=256)
        else:
            proj = _matmul(xt, od_w_in[j].astype(BF16), tm=1024, tn=512)
            y = _lru(proj, od_conv_w[j], od_conv_b[j], od_r_w[j], od_r_b[j], od_i_w[j], od_i_b[j],
                     od_lru_L[j], batch=batch, seq=seq, tm=512)
            xn, xn_b = _proj_ln([y], [od_w_out[j].astype(BF16)], xt, ln_mix_g[layer], ln_mix_b[layer], tm=256)
        ffn = _peer_block(xn, xn_b, peer_w_q[layer], peer_subkeys[layer], peer_u[layer], peer_v[layer])
        xt = _ple(xn, ffn, p[layer].reshape(t, -1), ln_ffn_g[layer], ln_ffn_b[layer], ple_w_gate[layer],
                  ple_b_gate[layer], ple_w_proj[layer], tm=256)
    return xt.reshape(batch, seq, d)
```

```python
import functools

import jax
import jax.numpy as jnp
from jax import lax
from jax.experimental import pallas as pl
from jax.experimental.pallas import tpu as pltpu

F32 = jnp.float32
BF16 = jnp.bfloat16
F8 = jnp.float8_e4m3fn
F8_RANGE = 224.0
F8_TINY = 1e-30

D_MODEL = 2048
DEPTH = 2
DN_ALPHA = (2.0 * DEPTH) ** 0.25
LN_EPS = 1e-5

RWKV_WIDTH = 1024
RWKV_HEAD = 64
RWKV_GN_EPS = 64e-5
RWKV_CHUNK = 64
GDN_WIDTH = 1024
GDN_HEAD = 128
GDN_HEADS = 8
GDN_CHUNK = 128
GDN_NORM_EPS = 1e-6
LRU_BLOCKS = 8
LRU_BLOCK = 256
LRU_C = 8.0
PEER_HEADS = 8
PEER_KEYS = 128
PEER_TOPK = 16

LANES = 128
SUBLANES = 8
VMEM_LIMIT = 56 * 1024 * 1024

EV_QKV0 = 0
EV_RKV0 = 3072
EV_Z0 = 6144
EV_LORA0 = 7168
EV_BA0 = 7552
EV_COLS = 7680


def _cparams(n_axes):
    return pltpu.CompilerParams(dimension_semantics=("arbitrary",) * n_axes,
                                vmem_limit_bytes=VMEM_LIMIT)


def _dot(a, b):
    a, b = a.astype(BF16), b.astype(BF16)
    if a.ndim == 3:
        return lax.dot_general(a, b, (((2,), (1,)), ((0,), (0,))), preferred_element_type=F32)
    return jnp.dot(a, b, preferred_element_type=F32)


def _dot_nt(a, b):
    a, b = a.astype(BF16), b.astype(BF16)
    if a.ndim == 3:
        return lax.dot_general(a, b, (((2,), (2,)), ((0,), (0,))), preferred_element_type=F32)
    return lax.dot_general(a, b, (((1,), (1,)), ((), ())), preferred_element_type=F32)


def _dot_tn(a, b):
    return _dot(jnp.swapaxes(a, -1, -2), b)


def _split(x):
    hi = x.astype(BF16)
    lo = (x - hi.astype(F32)).astype(BF16)
    return hi, lo


def _dot_lhs2(a, b):
    hi, lo = _split(a)
    return _dot(hi, b) + _dot(lo, b)


def _dot_rhs2(a, b):
    hi, lo = _split(b)
    return _dot(a, hi) + _dot(a, lo)


def _dot3(a, b):
    ah, al = _split(a)
    bh, bl = _split(b)
    return _dot(ah, bh) + _dot(al, bh) + _dot(ah, bl)


def _sigmoid(x):
    return 1.0 / (1.0 + jnp.exp(-x))


def _softplus(x):
    return jnp.maximum(x, 0.0) + jnp.log(1.0 + jnp.exp(-jnp.abs(x)))


def _gelu(x):
    return 0.5 * x * (1.0 + jnp.tanh(0.7978845608028654 * (x + 0.044715 * (x * x * x))))


def _iota2(shape, axis):
    return lax.broadcasted_iota(jnp.int32, shape, axis)


def _layer_norm(x, g, b):
    mu = jnp.mean(x, axis=-1, keepdims=True)
    xc = x - mu
    var = jnp.mean(xc * xc, axis=-1, keepdims=True)
    return xc * lax.rsqrt(var + LN_EPS) * g + b


def _shift_rows(x, halo, k):
    rolled = pltpu.roll(x, k, 0)
    hr = pltpu.roll(halo, k, 0)
    row = _iota2(halo.shape, 0)
    top = jnp.where(row < k, hr, rolled[0:SUBLANES])
    return jnp.concatenate([top, rolled[SUBLANES:]], axis=0)


def _tri_inv(x, n, top):
    row = _iota2((n, n), 0)
    col = _iota2((n, n), 1)
    eye = (row == col).astype(F32)
    x0 = jnp.where((row >> 3) == (col >> 3), x, 0.0)
    x2 = _dot(x0, x0)
    x4 = _dot(x2, x2)
    t = eye + x0
    t = t + _dot(t, x2)
    t = t + _dot(t, x4)
    s = 8
    while s < top:
        sh = s.bit_length() - 1
        off = ((row >> (sh + 1)) == (col >> (sh + 1))) & ((row >> sh) != (col >> sh))
        xo = jnp.where(off, x, 0.0)
        t = t + _dot(_dot(t, xo), t)
        s *= 2
    return t


def _mm_kernel(x_ref, w_ref, o_ref):
    o_ref[...] = jnp.dot(x_ref[...].astype(BF16), w_ref[...],
                         preferred_element_type=F32).astype(o_ref.dtype)


def _matmul(x, w, *, tm, tn, out_dtype=F32):
    m, k = x.shape
    n = w.shape[1]
    return pl.pallas_call(
        _mm_kernel, grid=(m // tm, n // tn),
        in_specs=[pl.BlockSpec((tm, k), lambda i, j: (i, 0)),
                  pl.BlockSpec((k, tn), lambda i, j: (0, j))],
        out_specs=pl.BlockSpec((tm, tn), lambda i, j: (i, j)),
        out_shape=jax.ShapeDtypeStruct((m, n), out_dtype),
        compiler_params=_cparams(2), name="matmul")(x, w)


def _proj_ln_kernel(*refs, n_in):
    ys = refs[:n_in]
    ws = refs[n_in:2 * n_in]
    x_ref, g_ref, b_ref, o_ref, ob_ref, o8_ref, r8_ref = refs[2 * n_in:]
    mix = _dot(ys[0][...], ws[0][...])
    for y_ref, w_ref in zip(ys[1:], ws[1:]):
        mix = mix + _dot(y_ref[...], w_ref[...])
    xn = _layer_norm(DN_ALPHA * x_ref[...] + mix, g_ref[...], b_ref[...])
    o_ref[...] = xn
    ob_ref[...] = xn.astype(BF16)
    amax = jnp.maximum(jnp.max(jnp.abs(xn), axis=-1, keepdims=True), F8_TINY)
    o8_ref[...] = (xn * (F8_RANGE / amax)).astype(F8)
    r8_ref[...] = jnp.broadcast_to(amax * (1.0 / F8_RANGE), r8_ref.shape)


def _proj_ln(ys, ws, x, g, b, *, tm):
    t, d = x.shape
    n_in = len(ys)
    in_specs = ([pl.BlockSpec((tm, y.shape[1]), lambda i: (i, 0)) for y in ys]
                + [pl.BlockSpec(w.shape, lambda i: (0, 0)) for w in ws]
                + [pl.BlockSpec((tm, d), lambda i: (i, 0)),
                   pl.BlockSpec((1, d), lambda i: (0, 0)),
                   pl.BlockSpec((1, d), lambda i: (0, 0))])
    return pl.pallas_call(
        functools.partial(_proj_ln_kernel, n_in=n_in), grid=(t // tm,),
        in_specs=in_specs,
        out_specs=[pl.BlockSpec((tm, d), lambda i: (i, 0))] * 3 + [pl.BlockSpec((tm, LANES), lambda i: (i, 0))],
        out_shape=[jax.ShapeDtypeStruct((t, d), F32), jax.ShapeDtypeStruct((t, d), BF16),
                   jax.ShapeDtypeStruct((t, d), F8), jax.ShapeDtypeStruct((t, LANES), F32)],
        compiler_params=_cparams(1), name="proj_ln")(*ys, *ws, x, g.reshape(1, d), b.reshape(1, d))


def _rwkv_pre_kernel(p_ref, ph_ref, l_ref, lh_ref, mu_ref, mul_ref, w0_ref, wup_ref, a0_ref, aup_ref,
                     gup_ref, kk_ref, ka_ref, rk_ref, bd_ref,
                     r_o, lw_o, k_o, v_o, a_o, b_o, bonus_o, g_o, *, tiles_per_seq):
    first = (pl.program_id(0) % tiles_per_seq) == 0
    w = RWKV_WIDTH

    def mixed(ref, href, m_ref, c0, c1):
        x = ref[:, c0:c1]
        hrow = jnp.where(first, 0.0, href[SUBLANES - 1:SUBLANES, c0:c1])
        prev = pltpu.roll(x, 1, 0)
        prev = jnp.where(_iota2(x.shape, 0) == 0, hrow, prev)
        return x + (prev - x) * m_ref[:, c0:c1]

    def segsum(x):
        bd = bd_ref[...]
        return jnp.concatenate(
            [_dot_lhs2(x[:, LANES * p:LANES * (p + 1)], bd) for p in range(w // LANES)], axis=1)

    r = mixed(p_ref, ph_ref, mu_ref, 0, w)
    k = mixed(p_ref, ph_ref, mu_ref, w, 2 * w)
    v = mixed(p_ref, ph_ref, mu_ref, 2 * w, 3 * w)
    wd = mixed(l_ref, lh_ref, mul_ref, 0, LANES)
    ad = mixed(l_ref, lh_ref, mul_ref, LANES, 2 * LANES)
    gd = mixed(l_ref, lh_ref, mul_ref, 2 * LANES, 3 * LANES)

    w_log = -_softplus(-(w0_ref[...] + _dot3(jnp.tanh(wd), wup_ref[...]))) - 0.5
    lw_o[...] = -jnp.exp(w_log)
    a = _sigmoid(a0_ref[...] + _dot3(ad, aup_ref[...]))
    g_o[...] = _dot3(_sigmoid(gd), gup_ref[...])
    kkr = k * kk_ref[...]
    kk = kkr * lax.rsqrt(segsum(kkr * kkr) + 1e-6)
    k2 = k * (1.0 + (a - 1.0) * ka_ref[...])
    bonus_o[...] = segsum(r * k2 * rk_ref[...]) * v
    r_o[...] = r
    k_o[...] = k2
    v_o[...] = v
    a_o[...] = -kk
    b_o[...] = kk * a


def _rwkv_pre(proj, seq, mu_rkv, mu_lora, w0, w_up, a0, a_up, g_up, k_k, k_a, r_k, bd64, *, tm):
    t = proj.shape[0]
    w = RWKV_WIDTH
    nh = tm // SUBLANES
    row = lambda a: a.reshape(1, -1)
    full = lambda a: pl.BlockSpec(a.shape, lambda i: (0, 0))
    args = [row(mu_rkv), row(mu_lora), row(w0), w_up, row(a0), a_up, g_up, row(k_k), row(k_a), row(r_k), bd64]
    in_specs = [pl.BlockSpec((tm, 3 * w), lambda i: (i, EV_RKV0 // (3 * w))),
                pl.BlockSpec((SUBLANES, 3 * w), lambda i: (jnp.maximum(i * nh - 1, 0), EV_RKV0 // (3 * w))),
                pl.BlockSpec((tm, 512), lambda i: (i, EV_LORA0 // 512)),
                pl.BlockSpec((SUBLANES, 512), lambda i: (jnp.maximum(i * nh - 1, 0), EV_LORA0 // 512))]
    in_specs += [full(a) for a in args]
    out_spec = pl.BlockSpec((tm, w), lambda i: (i, 0))
    return pl.pallas_call(
        functools.partial(_rwkv_pre_kernel, tiles_per_seq=seq // tm), grid=(t // tm,),
        in_specs=in_specs, out_specs=[out_spec] * 8,
        out_shape=[jax.ShapeDtypeStruct((t, w), F32)] * 8,
        compiler_params=_cparams(1), name="rwkv_pre")(proj, proj, proj, proj, *args)


def _rwkv_chunk_kernel(r_ref, lw_ref, k_ref, v_ref, a_ref, b_ref, bonus_ref, g_ref, lng_ref, lnb_ref,
                       bd_ref, o_ref, z_ref, *, n_chunks):
    L = RWKV_CHUNK

    @pl.when(pl.program_id(2) == 0)
    def _():
        z_ref[...] = jnp.zeros_like(z_ref)

    nc = n_chunks
    lane = _iota2((nc, L, LANES), 2)
    head0 = lane < RWKV_HEAD
    row = _iota2((2 * L, 2 * L), 0)
    col = _iota2((2 * L, 2 * L), 1)
    strict = row > col
    incl = row >= col
    eye = row == col
    tril = jnp.broadcast_to((_iota2((L, L), 0) >= _iota2((L, L), 1)).astype(BF16), (nc, L, L))
    bd = bd_ref[...]

    def stack2(x):
        return jnp.concatenate([jnp.where(head0, x, 0.0), jnp.where(head0, 0.0, x)], axis=1)

    r, lw, k, v, a, b = (ref[...].reshape(nc, L, LANES) for ref in (r_ref, lw_ref, k_ref, v_ref, a_ref, b_ref))
    cl = _dot_rhs2(tril, lw)
    cl_last = cl[:, L - 1:L, :]
    e_in = jnp.exp(cl)
    e_out = jnp.exp(-cl)
    e_end = jnp.exp(cl_last - cl)
    rt = stack2(r * e_in)
    at = stack2(a * jnp.exp(cl - lw))
    kt = stack2(k * e_out)
    bt = stack2(b * e_out)
    kc = stack2(k * e_end)
    bc = stack2(b * e_end)
    vs = stack2(v)
    aa = _dot_nt(jnp.concatenate([at, rt], axis=1), jnp.concatenate([bt, kt], axis=1))
    a_ab = jnp.where(strict, aa[:, 0:2 * L, 0:2 * L], 0.0)
    a_ak = jnp.where(strict, aa[:, 0:2 * L, 2 * L:4 * L], 0.0)
    a_rb = jnp.where(incl, aa[:, 2 * L:4 * L, 0:2 * L], 0.0)
    a_rk = jnp.where(incl, aa[:, 2 * L:4 * L, 2 * L:4 * L], 0.0)
    tm = _tri_inv(a_ab, 2 * L, L)
    a_hat = _dot(tm, at)
    u_v = _dot(tm, _dot(a_ak, vs))
    r_hat = rt + _dot(a_rb, a_hat)
    y_v = _dot(a_rb, u_v) + _dot(a_rk, vs)
    m = jnp.where(eye, jnp.exp(cl_last), 0.0) + _dot_tn(bc, a_hat)
    n = _dot_tn(bc, u_v) + _dot_tn(kc, vs)

    z = z_ref[...]
    ys = []
    for c in range(nc):
        yc2 = _dot(r_hat[c], z) + y_v[c]
        z = _dot(m[c], z) + n[c]
        ys.append(yc2[0:L] + yc2[L:2 * L])
    z_ref[...] = z
    y = jnp.concatenate(ys, axis=0)

    mean = _dot_lhs2(y, bd) * (1.0 / RWKV_HEAD)
    yc = y - mean
    var = _dot_lhs2(yc * yc, bd) * (1.0 / RWKV_HEAD)
    yn = yc * lax.rsqrt(var + RWKV_GN_EPS) * lng_ref[...] + lnb_ref[...]
    o_ref[...] = ((yn + bonus_ref[...]) * g_ref[...]).astype(o_ref.dtype)


def _rwkv_chunks(r, lw, k, v, a, b, bonus, g, ln_g, ln_b, bd64, *, batch, seq, tb):
    t, w = r.shape
    npairs = w // LANES
    nblk = seq // tb
    tok = pl.BlockSpec((tb, LANES), lambda bi, p, j: (bi * nblk + j, p))
    par = pl.BlockSpec((1, LANES), lambda bi, p, j: (0, p))
    return pl.pallas_call(
        functools.partial(_rwkv_chunk_kernel, n_chunks=tb // RWKV_CHUNK),
        grid=(batch, npairs, nblk),
        in_specs=[tok] * 8 + [par, par, pl.BlockSpec((LANES, LANES), lambda bi, p, j: (0, 0))],
        out_specs=tok,
        out_shape=jax.ShapeDtypeStruct((t, w), BF16),
        scratch_shapes=[pltpu.VMEM((LANES, LANES), F32)],
        compiler_params=_cparams(3), name="rwkv_chunks")(
            r, lw, k, v, a, b, bonus, g, ln_g.reshape(1, w), ln_b.reshape(1, w), bd64)


def _gdn_pre_kernel(p_ref, ph_ref, ba_ref, cw_ref, alog_ref, dtb_ref,
                    q_o, k_o, kb_o, vb_o, gb_o, *, tiles_per_seq):
    first = (pl.program_id(0) % tiles_per_seq) == 0
    w = GDN_WIDTH
    tm = p_ref.shape[0]
    ones = jnp.ones((LANES, LANES), BF16)

    def conv_silu(c0, c1):
        x = p_ref[:, c0:c1]
        halo = jnp.where(first, 0.0, ph_ref[:, c0:c1])
        out = x * cw_ref[3:4, c0:c1]
        for kshift in (1, 2, 3):
            out = out + _shift_rows(x, halo, kshift) * cw_ref[3 - kshift:4 - kshift, c0:c1]
        return out * _sigmoid(out)

    ba = ba_ref[...]
    beta = _sigmoid(ba)
    glog = -jnp.exp(alog_ref[...]) * _softplus(ba + dtb_ref[...])
    srow = _iota2((LANES, LANES), 0)
    for h in range(GDN_HEADS):
        sl = slice(GDN_HEAD * h, GDN_HEAD * (h + 1))
        beta_b = _dot_lhs2(beta, (srow == h).astype(BF16))
        gb_o[:, sl] = _dot_lhs2(glog, (srow == GDN_HEADS + h).astype(BF16))
        q = conv_silu(GDN_HEAD * h, GDN_HEAD * (h + 1))
        k = conv_silu(w + GDN_HEAD * h, w + GDN_HEAD * (h + 1))
        v = conv_silu(2 * w + GDN_HEAD * h, 2 * w + GDN_HEAD * (h + 1))
        qn = q * lax.rsqrt(_dot_lhs2(q * q, ones) + 1e-6) * (GDN_HEAD ** -0.5)
        kn = k * lax.rsqrt(_dot_lhs2(k * k, ones) + 1e-6)
        q_o[:, sl] = qn.astype(BF16)
        k_o[:, sl] = kn.astype(BF16)
        kb_o[:, sl] = (kn * beta_b).astype(BF16)
        vb_o[:, sl] = (v * beta_b).astype(BF16)
    del tm


def _gdn_pre(proj, seq, conv_w, a_log, dt_bias, *, tm):
    t = proj.shape[0]
    w = GDN_WIDTH
    nh = tm // SUBLANES
    pad = lambda a: jnp.zeros((1, LANES), F32).at[0, GDN_HEADS:2 * GDN_HEADS].set(a)
    full = lambda a: pl.BlockSpec(a.shape, lambda i: (0, 0))
    args = [conv_w, pad(a_log), pad(dt_bias)]
    out_spec = pl.BlockSpec((tm, w), lambda i: (i, 0))
    return pl.pallas_call(
        functools.partial(_gdn_pre_kernel, tiles_per_seq=seq // tm), grid=(t // tm,),
        in_specs=[pl.BlockSpec((tm, 3 * w), lambda i: (i, EV_QKV0 // (3 * w))),
                  pl.BlockSpec((SUBLANES, 3 * w), lambda i: (jnp.maximum(i * nh - 1, 0), EV_QKV0 // (3 * w))),
                  pl.BlockSpec((tm, LANES), lambda i: (i, EV_BA0 // LANES))] + [full(a) for a in args],
        out_specs=[out_spec] * 5,
        out_shape=[jax.ShapeDtypeStruct((t, w), BF16)] * 4 + [jax.ShapeDtypeStruct((t, w), F32)],
        compiler_params=_cparams(1), name="gdn_pre")(proj, proj, proj, *args)


def _gdn_chunk_kernel(q_ref, k_ref, kb_ref, vb_ref, gb_ref, z_ref, ng_ref, o_ref, s_ref, *, n_chunks):
    C = GDN_CHUNK

    @pl.when(pl.program_id(2) == 0)
    def _():
        s_ref[...] = jnp.zeros_like(s_ref)

    row = _iota2((C, C), 0)
    col = _iota2((C, C), 1)
    strict = row > col
    incl = row >= col
    eye = row == col
    nc = n_chunks
    tril = jnp.broadcast_to(incl.astype(BF16), (nc, C, C))

    q, k, kb, vb = (ref[...].astype(F32).reshape(nc, C, GDN_HEAD) for ref in (q_ref, k_ref, kb_ref, vb_ref))
    gc = _dot_rhs2(tril, gb_ref[...].reshape(nc, C, GDN_HEAD))
    gct = jnp.swapaxes(gc, 1, 2)
    dm = jnp.exp(jnp.where(incl, gc - gct, -1e30))
    egc = jnp.exp(gc)
    g_last = gc[:, C - 1:C, :]
    kk = _dot_nt(kb, k)
    qk = _dot_nt(q, k)
    m_mat = jnp.where(strict, kk * dm, 0.0)
    attn = qk * dm
    tinv = _tri_inv(-m_mat, C, C)
    u = _dot(tinv, vb)
    w = _dot(tinv, kb * egc)
    q_hat = q * egc - _dot(attn, w)
    o_v = _dot(attn, u)
    kd = k * jnp.exp(g_last - gc)
    mm = jnp.where(eye, jnp.exp(g_last), 0.0) - _dot_tn(kd, w)
    nn = _dot_tn(kd, u)

    s = s_ref[...]
    os_ = []
    for c in range(nc):
        os_.append(_dot(q_hat[c], s) + o_v[c])
        s = _dot(mm[c], s) + nn[c]
    s_ref[...] = s
    o = jnp.concatenate(os_, axis=0)
    o = o * lax.rsqrt(jnp.mean(o * o, axis=-1, keepdims=True) + GDN_NORM_EPS) * ng_ref[...]
    zz = z_ref[...]
    o_ref[...] = (o * (zz * _sigmoid(zz))).astype(o_ref.dtype)


def _gdn_chunks(q, k, kb, vb, gb, proj, norm_g, *, batch, seq, tb):
    t, w = q.shape
    nblk = seq // tb
    tok = pl.BlockSpec((tb, GDN_HEAD), lambda bi, h, j: (bi * nblk + j, h))
    zspec = pl.BlockSpec((tb, GDN_HEAD), lambda bi, h, j: (bi * nblk + j, EV_Z0 // GDN_HEAD + h))
    return pl.pallas_call(
        functools.partial(_gdn_chunk_kernel, n_chunks=tb // GDN_CHUNK),
        grid=(batch, GDN_HEADS, nblk),
        in_specs=[tok] * 5 + [zspec, pl.BlockSpec((1, GDN_HEAD), lambda bi, h, j: (0, 0))],
        out_specs=tok,
        out_shape=jax.ShapeDtypeStruct((t, w), BF16),
        scratch_shapes=[pltpu.VMEM((GDN_HEAD, GDN_HEAD), F32)],
        compiler_params=_cparams(3), name="gdn_chunks")(q, k, kb, vb, gb, proj, norm_g.reshape(1, GDN_HEAD))


def _head_block_ones():
    i = jnp.arange(LANES)
    return ((i[:, None] // RWKV_HEAD) == (i[None, :] // RWKV_HEAD)).astype(BF16)


def _pad_rows(a, rows):
    return jnp.zeros((rows,) + a.shape[1:], a.dtype).at[:a.shape[0]].set(a)


def _even_mixers(proj, batch, seq, prm):
    bd64 = _head_block_ones()
    mu = prm['ev_shift_mu']
    w = RWKV_WIDTH
    mu_lora = jnp.zeros((512,), F32)
    mu_lora = mu_lora.at[0:96].set(mu[3 * w:3 * w + 96]).at[128:224].set(mu[3 * w + 96:3 * w + 192])
    mu_lora = mu_lora.at[256:320].set(mu[3 * w + 192:3 * w + 256])
    pre = _rwkv_pre(proj, seq, mu[:3 * w], mu_lora, prm['ev_rwkv_w0'], _pad_rows(prm['ev_rwkv_w_up'], LANES),
                    prm['ev_rwkv_a0'], _pad_rows(prm['ev_rwkv_a_up'], LANES),
                    _pad_rows(prm['ev_rwkv_g_up'], LANES), prm['ev_rwkv_k_k'], prm['ev_rwkv_k_a'],
                    prm['ev_rwkv_r_k'].reshape(-1), bd64, tm=256)
    y_a = _rwkv_chunks(*pre, prm['ev_rwkv_ln_g'], prm['ev_rwkv_ln_b'], bd64, batch=batch, seq=seq, tb=1024)
    gpre = _gdn_pre(proj, seq, prm['ev_gdn_conv'], prm['ev_gdn_a_log'], prm['ev_gdn_dt_bias'], tm=256)
    y_b = _gdn_chunks(*gpre, proj, prm['ev_gdn_norm_g'], batch=batch, seq=seq, tb=2048)
    return y_a, y_b


def _lru_kernel(x_ref, xh_ref, gate_ref, cw_ref, cb_ref, rw_ref, rb_ref, iw_ref, ib_ref, l_ref,
                o_ref, h_ref):
    j = pl.program_id(2)
    tm = x_ref.shape[0]

    @pl.when(j == 0)
    def _():
        h_ref[...] = jnp.zeros_like(h_ref)

    x = x_ref[...]
    halo = jnp.where(j == 0, 0.0, xh_ref[...])
    xc = x * cw_ref[3:4, :] + cb_ref[...]
    for kshift in (1, 2, 3):
        xc = xc + _shift_rows(x, halo, kshift) * cw_ref[3 - kshift:4 - kshift, :]
    r = _sigmoid(_dot(xc, rw_ref[0]) + rb_ref[...])
    i = _sigmoid(_dot(xc, iw_ref[0]) + ib_ref[...])
    log_a = -LRU_C * r * _softplus(-l_ref[...])
    a = jnp.exp(log_a)
    b = jnp.sqrt(1.0 - jnp.exp(2.0 * log_a)) * (i * xc)
    row = _iota2(a.shape, 0)
    s = 1
    while s < tm:
        keep = row >= s
        a_sh = jnp.where(keep, pltpu.roll(a, s, 0), 1.0)
        b_sh = jnp.where(keep, pltpu.roll(b, s, 0), 0.0)
        b = a * b_sh + b
        a = a * a_sh
        s *= 2
    h = b + a * h_ref[0:1, :]
    h_ref[...] = jnp.broadcast_to(h[tm - 1:tm, :], h_ref.shape)
    o_ref[...] = (h * _gelu(gate_ref[...])).astype(o_ref.dtype)


def _lru(proj, conv_w, conv_b, r_w, r_b, i_w, i_b, lru_l, *, batch, seq, tm):
    t = proj.shape[0]
    w = LRU_BLOCKS * LRU_BLOCK
    nblk = seq // tm
    nh = tm // SUBLANES
    vec = pl.BlockSpec((1, LRU_BLOCK), lambda bi, g, j: (0, g))
    mat = pl.BlockSpec((1, LRU_BLOCK, LRU_BLOCK), lambda bi, g, j: (g, 0, 0))
    row = lambda a: a.reshape(1, w)
    return pl.pallas_call(
        _lru_kernel, grid=(batch, LRU_BLOCKS, nblk),
        in_specs=[pl.BlockSpec((tm, LRU_BLOCK), lambda bi, g, j: (bi * nblk + j, LRU_BLOCKS + g)),
                  pl.BlockSpec((SUBLANES, LRU_BLOCK),
                               lambda bi, g, j: (jnp.maximum((bi * nblk + j) * nh - 1, 0), LRU_BLOCKS + g)),
                  pl.BlockSpec((tm, LRU_BLOCK), lambda bi, g, j: (bi * nblk + j, g)),
                  pl.BlockSpec((4, LRU_BLOCK), lambda bi, g, j: (0, g)), vec, mat, vec, mat, vec, vec],
        out_specs=pl.BlockSpec((tm, LRU_BLOCK), lambda bi, g, j: (bi * nblk + j, g)),
        out_shape=jax.ShapeDtypeStruct((t, w), BF16),
        scratch_shapes=[pltpu.VMEM((SUBLANES, LRU_BLOCK), F32)],
        compiler_params=_cparams(3), name="rglru")(
            proj, proj, proj, conv_w, row(conv_b), r_w.astype(BF16), row(r_b), i_w.astype(BF16), row(i_b),
            row(lru_l))


def _oddeven_merge_sort_network(n):
    pairs = []

    def merge(lo, hi, r):
        step = 2 * r
        if step < hi - lo:
            merge(lo, hi, step)
            merge(lo + r, hi, step)
            pairs.extend((i, i + r) for i in range(lo + r, hi - r, step))
        else:
            pairs.append((lo, lo + r))

    def sort(lo, hi):
        if hi - lo >= 1:
            mid = lo + (hi - lo) // 2
            sort(lo, mid)
            sort(mid + 1, hi)
            merge(lo, hi, 1)

    sort(0, n - 1)
    return tuple(pairs)


_SORT16 = _oddeven_merge_sort_network(PEER_KEYS // SUBLANES)


def _peer_topk_kernel(q_ref, sk_ref, id_o, gate_o, v_ref, i_ref, s_ref, c_ref):
    tm = q_ref.shape[0]
    neg = -jnp.inf
    ngrp = PEER_KEYS // SUBLANES
    sub = _iota2((SUBLANES, tm), 0).astype(F32)

    for half in range(2):
        qh = q_ref[:, PEER_KEYS * half:PEER_KEYS * (half + 1)]
        ah, al = _split(sk_ref[0, half])
        bh, bl = _split(qh)
        nt = lambda a, b: lax.dot_general(a, b, (((1,), (1,)), ((), ())), preferred_element_type=F32)
        scores = nt(ah, bh) + nt(al, bh) + nt(ah, bl)

        rows = [scores[SUBLANES * a:SUBLANES * (a + 1), :] for a in range(ngrp)]
        keys = [sub + float(SUBLANES * a) for a in range(ngrp)]
        for i, j in _SORT16:
            swap = rows[j] > rows[i]
            rows[i], rows[j] = jnp.maximum(rows[i], rows[j]), jnp.minimum(rows[i], rows[j])
            keys[i], keys[j] = jnp.where(swap, keys[j], keys[i]), jnp.where(swap, keys[i], keys[j])
        for kk in range(PEER_TOPK):
            m = jnp.max(rows[0], axis=0, keepdims=True)
            idx = jnp.min(jnp.where(rows[0] == m, keys[0], 1e9), axis=0, keepdims=True)
            v_ref[half, kk:kk + 1, :] = m
            i_ref[half, kk:kk + 1, :] = idx
            win = keys[0] == idx
            for a in range(PEER_TOPK - 1 - kk):
                rows[a] = jnp.where(win, rows[a + 1], rows[a])
                keys[a] = jnp.where(win, keys[a + 1], keys[a])

    v1, v2 = v_ref[0], v_ref[1]
    i1, i2 = i_ref[0], i_ref[1]
    cands = [v1[0:1] + v2]
    ids = [i1[0:1] * PEER_KEYS + i2]
    for a in range(1, 8):
        cands.append(v1[a:a + 1] + v2[0:8])
        ids.append(i1[a:a + 1] * PEER_KEYS + i2[0:8])
    cands.append(v1[8:16] + v2[0:1])
    ids.append(i1[8:16] * PEER_KEYS + i2[0:1])
    cand = jnp.concatenate(cands, axis=0)
    c_ref[...] = jnp.concatenate(ids, axis=0)
    piota = _iota2(cand.shape, 0).astype(F32)

    def pick2(kk, c):
        m = jnp.max(c, axis=0, keepdims=True)
        pos = jnp.min(jnp.where(c == m, piota, 1e9), axis=0, keepdims=True)
        hit = piota == pos
        s_ref[pl.ds(kk, 1), :] = m
        id_o[0, pl.ds(kk, 1), :] = jnp.max(jnp.where(hit, c_ref[...], -1.0), axis=0, keepdims=True)
        return jnp.where(hit, neg, c)

    lax.fori_loop(0, PEER_TOPK, pick2, cand)
    top = s_ref[...]
    e = jnp.exp(top - top[0:1])
    gate_o[0] = e / jnp.sum(e, axis=0, keepdims=True)


def _peer_topk(q, subkeys, *, tm):
    t = q.shape[0]
    out = pl.BlockSpec((1, PEER_TOPK, tm), lambda i, h: (h, 0, i))
    return pl.pallas_call(
        _peer_topk_kernel, grid=(t // tm, PEER_HEADS),
        in_specs=[pl.BlockSpec((tm, 2 * PEER_KEYS), lambda i, h: (i, h)),
                  pl.BlockSpec((1, 2, PEER_KEYS, PEER_KEYS), lambda i, h: (h, 0, 0, 0))],
        out_specs=[out, out],
        out_shape=[jax.ShapeDtypeStruct((PEER_HEADS, PEER_TOPK, t), F32)] * 2,
        scratch_shapes=[pltpu.VMEM((2, PEER_TOPK, tm), F32), pltpu.VMEM((2, PEER_TOPK, tm), F32),
                        pltpu.VMEM((PEER_TOPK, tm), F32), pltpu.VMEM((80, tm), F32)],
        compiler_params=_cparams(2), name="peer_topk")(q, subkeys)


GATE_TILE = 64
GATE_STRIDE = GATE_TILE + SUBLANES


def _peer_gates_kernel(i1_ref, i2_ref, g_ref, o_ref, s_ref):
    ciota = _iota2((PEER_KEYS, LANES), 0).astype(F32)

    def body(t, carry):
        i1 = i1_ref[pl.ds(t, 1), :]
        i2 = i2_ref[pl.ds(t, 1), :]
        g = g_ref[pl.ds(t, 1), :]
        a = jnp.where(ciota == i1, g, 0.0)
        bt = jnp.where(ciota == i2, 1.0, 0.0)
        s_ref[pl.ds(t, PEER_KEYS, stride=GATE_STRIDE), :] = _dot_nt(a, bt)
        return carry

    lax.fori_loop(0, GATE_TILE, body, 0, unroll=16)
    for c in range(PEER_KEYS):
        o_ref[c] = s_ref[GATE_STRIDE * c:GATE_STRIDE * c + GATE_TILE, :].astype(o_ref.dtype)


def _peer_gates(i1, i2, gates):
    t = i1.shape[0]
    tok = pl.BlockSpec((GATE_TILE, LANES), lambda i: (i, 0))
    return pl.pallas_call(
        _peer_gates_kernel, grid=(t // GATE_TILE,),
        in_specs=[tok, tok, tok],
        out_specs=pl.BlockSpec((PEER_KEYS, GATE_TILE, PEER_KEYS), lambda i: (0, i, 0)),
        out_shape=jax.ShapeDtypeStruct((PEER_KEYS, t, PEER_KEYS), BF16),
        scratch_shapes=[pltpu.VMEM((PEER_KEYS * GATE_STRIDE, PEER_KEYS), F32)],
        compiler_params=_cparams(1), name="peer_gates")(i1, i2, gates)


PEER_STEP = 8 * PEER_KEYS

def _peer_ffn_kernel(sc_ref, x_ref, xr_ref, ut_ref, v_ref, g_ref, o_ref):
    @pl.when(pl.program_id(1) == 0)
    def _():
        o_ref[...] = jnp.zeros_like(o_ref)

    nslab = PEER_STEP // PEER_KEYS
    xr = xr_ref[...] * sc_ref[0]
    h = jnp.dot(x_ref[...], ut_ref[...], preferred_element_type=F32)
    wgt = jnp.concatenate(
        [_gelu(h[:, PEER_KEYS * c:PEER_KEYS * (c + 1)] * xr) * g_ref[c].astype(F32) for c in range(nslab)], axis=1)
    amax = jnp.maximum(jnp.max(jnp.abs(wgt), axis=-1, keepdims=True), F8_TINY)
    w8 = (wgt * (F8_RANGE / amax)).astype(F8)
    part = jnp.dot(w8, v_ref[...], preferred_element_type=F32)
    o_ref[...] += part * (amax * (sc_ref[1] / F8_RANGE))


def _peer_ffn(x8, xr, ut8, v8, descale, gmat, *, tm):
    t, d = x8.shape
    ne = ut8.shape[1]
    return pl.pallas_call(
        _peer_ffn_kernel, grid=(t // tm, ne // PEER_STEP),
        in_specs=[pl.BlockSpec(memory_space=pltpu.SMEM),
                  pl.BlockSpec((tm, d), lambda i, j: (i, 0)),
                  pl.BlockSpec((tm, LANES), lambda i, j: (i, 0)),
                  pl.BlockSpec((d, PEER_STEP), lambda i, j: (0, j)),
                  pl.BlockSpec((PEER_STEP, d), lambda i, j: (j, 0)),
                  pl.BlockSpec((PEER_STEP // PEER_KEYS, tm, PEER_KEYS), lambda i, j: (j, i, 0))],
        out_specs=pl.BlockSpec((tm, d), lambda i, j: (i, 0)),
        out_shape=jax.ShapeDtypeStruct((t, d), F32),
        compiler_params=_cparams(2), name="peer_ffn")(descale, x8, xr, ut8, v8, gmat)


def _ple_kernel(x_ref, f_ref, p_ref, g_ref, b_ref, wg_ref, bg_ref, wp_ref, o_ref):
    x2 = _layer_norm(DN_ALPHA * x_ref[...] + f_ref[...], g_ref[...], b_ref[...])
    gate = _sigmoid(_dot(x2, wg_ref[...]) + bg_ref[...])
    o_ref[...] = x2 + gate * _dot(p_ref[...], wp_ref[...])


def _ple(x, ffn, p, ln_g, ln_b, w_gate, b_gate, w_proj, *, tm):
    t, d = x.shape
    pd = p.shape[1]
    tok = pl.BlockSpec((tm, d), lambda i: (i, 0))
    vec = pl.BlockSpec((1, d), lambda i: (0, 0))
    return pl.pallas_call(
        _ple_kernel, grid=(t // tm,),
        in_specs=[tok, tok, pl.BlockSpec((tm, pd), lambda i: (i, 0)), vec, vec,
                  pl.BlockSpec((d, d), lambda i: (0, 0)), vec, pl.BlockSpec((pd, d), lambda i: (0, 0))],
        out_specs=tok, out_shape=jax.ShapeDtypeStruct((t, d), F32),
        compiler_params=_cparams(1), name="ple")(
            x, ffn, p, ln_g.reshape(1, d), ln_b.reshape(1, d), w_gate.astype(BF16), b_gate.reshape(1, d),
            w_proj.astype(BF16))


def _fp8_table(tab):
    amax = jnp.maximum(jnp.max(jnp.abs(tab)), F8_TINY)
    return (tab * (F8_RANGE / amax)).astype(F8), amax * (1.0 / F8_RANGE)


def _peer_block(xn_b, x8, xr, w_q, subkeys, u_tab, v_tab):
    t = xn_b.shape[0]
    q = _matmul(xn_b, w_q.astype(BF16), tm=1024, tn=512)
    ids, gates = _peer_topk(q, subkeys, tm=256)
    to_tok = lambda a: a.reshape(PEER_HEADS * PEER_TOPK, t).T
    ids = to_tok(ids)
    i1 = jnp.floor(ids * (1.0 / PEER_KEYS))
    i2 = ids - PEER_KEYS * i1
    gmat = _peer_gates(i1, i2, to_tok(gates))
    ut8, du = _fp8_table(u_tab.T)
    v8, dv = _fp8_table(v_tab)
    return _peer_ffn(x8, xr, ut8, v8, jnp.stack([du, dv]), gmat, tm=1024)


def _even_in_weight(w_in):
    w = RWKV_WIDTH
    rk = 3 * w
    lora = lambda c0, n: jnp.pad(w_in[:, c0:c0 + n], ((0, 0), (0, LANES - n)))
    g0 = rk + 256
    cols = [w_in[:, g0:g0 + 3 * GDN_WIDTH],
            w_in[:, 0:rk],
            w_in[:, g0 + 3 * GDN_WIDTH:g0 + 4 * GDN_WIDTH],
            lora(rk, 96), lora(rk + 96, 96), lora(rk + 192, 64),
            lora(g0 + 4 * GDN_WIDTH, 2 * GDN_HEADS)]
    return jnp.concatenate(cols, axis=1).astype(BF16)


def kernel(x, p, ev_w_in, ev_shift_mu, ev_rwkv_w0, ev_rwkv_w_up, ev_rwkv_a0, ev_rwkv_a_up, ev_rwkv_g_up, ev_rwkv_k_k, ev_rwkv_k_a, ev_rwkv_r_k, ev_rwkv_ln_g, ev_rwkv_ln_b, ev_gdn_conv, ev_gdn_a_log, ev_gdn_dt_bias, ev_gdn_norm_g, ev_w_out, od_w_in, od_conv_w, od_conv_b, od_r_w, od_r_b, od_i_w, od_i_b, od_lru_L, od_w_out, ln_mix_g, ln_mix_b, peer_w_q, peer_subkeys, peer_u, peer_v, ln_ffn_g, ln_ffn_b, ple_w_proj, ple_w_gate, ple_b_gate):
    batch, seq, d = x.shape
    t = batch * seq
    xt = x.reshape(t, d)
    ev = dict(ev_shift_mu=ev_shift_mu, ev_rwkv_w0=ev_rwkv_w0, ev_rwkv_w_up=ev_rwkv_w_up, ev_rwkv_a0=ev_rwkv_a0,
              ev_rwkv_a_up=ev_rwkv_a_up, ev_rwkv_g_up=ev_rwkv_g_up, ev_rwkv_k_k=ev_rwkv_k_k,
              ev_rwkv_k_a=ev_rwkv_k_a, ev_rwkv_r_k=ev_rwkv_r_k, ev_rwkv_ln_g=ev_rwkv_ln_g,
              ev_rwkv_ln_b=ev_rwkv_ln_b, ev_gdn_conv=ev_gdn_conv, ev_gdn_a_log=ev_gdn_a_log,
              ev_gdn_dt_bias=ev_gdn_dt_bias, ev_gdn_norm_g=ev_gdn_norm_g)
    for layer in range(p.shape[0]):
        j = layer // 2
        if layer % 2 == 0:
            proj = _matmul(xt, _even_in_weight(ev_w_in[j]), tm=1024, tn=512)
            y_a, y_b = _even_mixers(proj, batch, seq, {k: v[j] for k, v in ev.items()})
            w_out = ev_w_out[j].astype(BF16)
            xn, xn_b, x8, xr = _proj_ln([y_a, y_b], [w_out[:RWKV_WIDTH], w_out[RWKV_WIDTH:]], xt,
                                        ln_mix_g[layer], ln_mix_b[layer], tm=256)
        else:
            proj = _matmul(xt, od_w_in[j].astype(BF16), tm=1024, tn=512)
            y = _lru(proj, od_conv_w[j], od_conv_b[j], od_r_w[j], od_r_b[j], od_i_w[j], od_i_b[j],
                     od_lru_L[j], batch=batch, seq=seq, tm=512)
            xn, xn_b, x8, xr = _proj_ln([y], [od_w_out[j].astype(BF16)], xt, ln_mix_g[layer], ln_mix_b[layer],
                                        tm=256)
        ffn = _peer_block(xn_b, x8, xr, peer_w_q[layer], peer_subkeys[layer], peer_u[layer], peer_v[layer])
        xt = _ple(xn, ffn, p[layer].reshape(t, -1), ln_ffn_g[layer], ln_ffn_b[layer], ple_w_gate[layer],
                  ple_b_gate[layer], ple_w_proj[layer], tm=256)
    return xt.reshape(batch, seq, d)
```

```python
import functools

import jax
import jax.numpy as jnp
from jax import lax
from jax.experimental import pallas as pl
from jax.experimental.pallas import tpu as pltpu

F32 = jnp.float32
BF16 = jnp.bfloat16
F8 = jnp.float8_e4m3fn
F8_RANGE = 224.0
F8_TINY = 1e-30

D_MODEL = 2048
DEPTH = 2
DN_ALPHA = (2.0 * DEPTH) ** 0.25
LN_EPS = 1e-5

RWKV_WIDTH = 1024
RWKV_HEAD = 64
RWKV_GN_EPS = 64e-5
RWKV_CHUNK = 64
GDN_WIDTH = 1024
GDN_HEAD = 128
GDN_HEADS = 8
GDN_CHUNK = 128
GDN_NORM_EPS = 1e-6
LRU_BLOCKS = 8
LRU_BLOCK = 256
LRU_C = 8.0
PEER_HEADS = 8
PEER_KEYS = 128
PEER_TOPK = 16

LANES = 128
SUBLANES = 8
VMEM_LIMIT = 56 * 1024 * 1024

EV_QKV0 = 0
EV_RKV0 = 3072
EV_Z0 = 6144
EV_LORA0 = 7168
EV_BA0 = 7552
EV_COLS = 7680


def _cparams(n_axes):
    return pltpu.CompilerParams(dimension_semantics=("arbitrary",) * n_axes,
                                vmem_limit_bytes=VMEM_LIMIT)


def _dot(a, b):
    a, b = a.astype(BF16), b.astype(BF16)
    if a.ndim == 3:
        return lax.dot_general(a, b, (((2,), (1,)), ((0,), (0,))), preferred_element_type=F32)
    return jnp.dot(a, b, preferred_element_type=F32)


def _dot_nt(a, b):
    a, b = a.astype(BF16), b.astype(BF16)
    if a.ndim == 3:
        return lax.dot_general(a, b, (((2,), (2,)), ((0,), (0,))), preferred_element_type=F32)
    return lax.dot_general(a, b, (((1,), (1,)), ((), ())), preferred_element_type=F32)


def _dot_tn(a, b):
    return _dot(jnp.swapaxes(a, -1, -2), b)


def _split(x):
    hi = x.astype(BF16)
    lo = (x - hi.astype(F32)).astype(BF16)
    return hi, lo


def _dot_lhs2(a, b):
    hi, lo = _split(a)
    return _dot(hi, b) + _dot(lo, b)


def _dot_rhs2(a, b):
    hi, lo = _split(b)
    return _dot(a, hi) + _dot(a, lo)


def _dot3(a, b):
    ah, al = _split(a)
    bh, bl = _split(b)
    return _dot(ah, bh) + _dot(al, bh) + _dot(ah, bl)


def _sigmoid(x):
    return 1.0 / (1.0 + jnp.exp(-x))


def _softplus(x):
    return jnp.maximum(x, 0.0) + jnp.log(1.0 + jnp.exp(-jnp.abs(x)))


def _gelu(x):
    return 0.5 * x * (1.0 + jnp.tanh(0.7978845608028654 * (x + 0.044715 * (x * x * x))))


def _iota2(shape, axis):
    return lax.broadcasted_iota(jnp.int32, shape, axis)


def _layer_norm(x, g, b):
    mu = jnp.mean(x, axis=-1, keepdims=True)
    xc = x - mu
    var = jnp.mean(xc * xc, axis=-1, keepdims=True)
    return xc * lax.rsqrt(var + LN_EPS) * g + b


def _shift_rows(x, halo, k):
    rolled = pltpu.roll(x, k, 0)
    hr = pltpu.roll(halo, k, 0)
    row = _iota2(halo.shape, 0)
    top = jnp.where(row < k, hr, rolled[0:SUBLANES])
    return jnp.concatenate([top, rolled[SUBLANES:]], axis=0)


def _tri_inv(x, n, top):
    row = _iota2((n, n), 0)
    col = _iota2((n, n), 1)
    eye = (row == col).astype(F32)
    x0 = jnp.where((row >> 3) == (col >> 3), x, 0.0)
    x2 = _dot(x0, x0)
    x4 = _dot(x2, x2)
    t = eye + x0
    t = t + _dot(t, x2)
    t = t + _dot(t, x4)
    s = 8
    while s < top:
        sh = s.bit_length() - 1
        off = ((row >> (sh + 1)) == (col >> (sh + 1))) & ((row >> sh) != (col >> sh))
        xo = jnp.where(off, x, 0.0)
        t = t + _dot(_dot(t, xo), t)
        s *= 2
    return t


def _mm_kernel(x_ref, w_ref, o_ref):
    o_ref[...] = jnp.dot(x_ref[...].astype(BF16), w_ref[...],
                         preferred_element_type=F32).astype(o_ref.dtype)


def _matmul(x, w, *, tm, tn, out_dtype=F32):
    m, k = x.shape
    n = w.shape[1]
    return pl.pallas_call(
        _mm_kernel, grid=(m // tm, n // tn),
        in_specs=[pl.BlockSpec((tm, k), lambda i, j: (i, 0)),
                  pl.BlockSpec((k, tn), lambda i, j: (0, j))],
        out_specs=pl.BlockSpec((tm, tn), lambda i, j: (i, j)),
        out_shape=jax.ShapeDtypeStruct((m, n), out_dtype),
        compiler_params=_cparams(2), name="matmul")(x, w)


def _proj_ln_kernel(*refs, n_in):
    ys = refs[:n_in]
    ws = refs[n_in:2 * n_in]
    x_ref, g_ref, b_ref, o_ref, ob_ref, o8_ref, r8_ref = refs[2 * n_in:]
    mix = _dot(ys[0][...], ws[0][...])
    for y_ref, w_ref in zip(ys[1:], ws[1:]):
        mix = mix + _dot(y_ref[...], w_ref[...])
    xn = _layer_norm(DN_ALPHA * x_ref[...] + mix, g_ref[...], b_ref[...])
    o_ref[...] = xn
    ob_ref[...] = xn.astype(BF16)
    amax = jnp.maximum(jnp.max(jnp.abs(xn), axis=-1, keepdims=True), F8_TINY)
    o8_ref[...] = (xn * (F8_RANGE / amax)).astype(F8)
    r8_ref[...] = jnp.broadcast_to(amax * (1.0 / F8_RANGE), r8_ref.shape)


def _proj_ln(ys, ws, x, g, b, *, tm):
    t, d = x.shape
    n_in = len(ys)
    in_specs = ([pl.BlockSpec((tm, y.shape[1]), lambda i: (i, 0)) for y in ys]
                + [pl.BlockSpec(w.shape, lambda i: (0, 0)) for w in ws]
                + [pl.BlockSpec((tm, d), lambda i: (i, 0)),
                   pl.BlockSpec((1, d), lambda i: (0, 0)),
                   pl.BlockSpec((1, d), lambda i: (0, 0))])
    return pl.pallas_call(
        functools.partial(_proj_ln_kernel, n_in=n_in), grid=(t // tm,),
        in_specs=in_specs,
        out_specs=[pl.BlockSpec((tm, d), lambda i: (i, 0))] * 3 + [pl.BlockSpec((tm, LANES), lambda i: (i, 0))],
        out_shape=[jax.ShapeDtypeStruct((t, d), F32), jax.ShapeDtypeStruct((t, d), BF16),
                   jax.ShapeDtypeStruct((t, d), F8), jax.ShapeDtypeStruct((t, LANES), F32)],
        compiler_params=_cparams(1), name="proj_ln")(*ys, *ws, x, g.reshape(1, d), b.reshape(1, d))


def _rwkv_pre_kernel(p_ref, ph_ref, l_ref, lh_ref, mu_ref, mul_ref, w0_ref, wup_ref, a0_ref, aup_ref,
                     gup_ref, kk_ref, ka_ref, rk_ref, bd_ref,
                     r_o, lw_o, k_o, v_o, a_o, b_o, bonus_o, g_o, *, tiles_per_seq):
    first = (pl.program_id(0) % tiles_per_seq) == 0
    w = RWKV_WIDTH

    def mixed(ref, href, m_ref, c0, c1):
        x = ref[:, c0:c1]
        hrow = jnp.where(first, 0.0, href[SUBLANES - 1:SUBLANES, c0:c1])
        prev = pltpu.roll(x, 1, 0)
        prev = jnp.where(_iota2(x.shape, 0) == 0, hrow, prev)
        return x + (prev - x) * m_ref[:, c0:c1]

    def segsum(x):
        bd = bd_ref[...]
        return jnp.concatenate(
            [_dot_lhs2(x[:, LANES * p:LANES * (p + 1)], bd) for p in range(w // LANES)], axis=1)

    r = mixed(p_ref, ph_ref, mu_ref, 0, w)
    k = mixed(p_ref, ph_ref, mu_ref, w, 2 * w)
    v = mixed(p_ref, ph_ref, mu_ref, 2 * w, 3 * w)
    wd = mixed(l_ref, lh_ref, mul_ref, 0, LANES)
    ad = mixed(l_ref, lh_ref, mul_ref, LANES, 2 * LANES)
    gd = mixed(l_ref, lh_ref, mul_ref, 2 * LANES, 3 * LANES)

    w_log = -_softplus(-(w0_ref[...] + _dot3(jnp.tanh(wd), wup_ref[...]))) - 0.5
    lw_o[...] = -jnp.exp(w_log)
    a = _sigmoid(a0_ref[...] + _dot3(ad, aup_ref[...]))
    g_o[...] = _dot3(_sigmoid(gd), gup_ref[...])
    kkr = k * kk_ref[...]
    kk = kkr * lax.rsqrt(segsum(kkr * kkr) + 1e-6)
    k2 = k * (1.0 + (a - 1.0) * ka_ref[...])
    bonus_o[...] = segsum(r * k2 * rk_ref[...]) * v
    r_o[...] = r
    k_o[...] = k2
    v_o[...] = v
    a_o[...] = -kk
    b_o[...] = kk * a


def _rwkv_pre(proj, seq, mu_rkv, mu_lora, w0, w_up, a0, a_up, g_up, k_k, k_a, r_k, bd64, *, tm):
    t = proj.shape[0]
    w = RWKV_WIDTH
    nh = tm // SUBLANES
    row = lambda a: a.reshape(1, -1)
    full = lambda a: pl.BlockSpec(a.shape, lambda i: (0, 0))
    args = [row(mu_rkv), row(mu_lora), row(w0), w_up, row(a0), a_up, g_up, row(k_k), row(k_a), row(r_k), bd64]
    in_specs = [pl.BlockSpec((tm, 3 * w), lambda i: (i, EV_RKV0 // (3 * w))),
                pl.BlockSpec((SUBLANES, 3 * w), lambda i: (jnp.maximum(i * nh - 1, 0), EV_RKV0 // (3 * w))),
                pl.BlockSpec((tm, 512), lambda i: (i, EV_LORA0 // 512)),
                pl.BlockSpec((SUBLANES, 512), lambda i: (jnp.maximum(i * nh - 1, 0), EV_LORA0 // 512))]
    in_specs += [full(a) for a in args]
    out_spec = pl.BlockSpec((tm, w), lambda i: (i, 0))
    return pl.pallas_call(
        functools.partial(_rwkv_pre_kernel, tiles_per_seq=seq // tm), grid=(t // tm,),
        in_specs=in_specs, out_specs=[out_spec] * 8,
        out_shape=[jax.ShapeDtypeStruct((t, w), F32)] * 8,
        compiler_params=_cparams(1), name="rwkv_pre")(proj, proj, proj, proj, *args)


def _rwkv_chunk_kernel(r_ref, lw_ref, k_ref, v_ref, a_ref, b_ref, bonus_ref, g_ref, lng_ref, lnb_ref,
                       bd_ref, o_ref, z_ref, *, n_chunks):
    L = RWKV_CHUNK

    @pl.when(pl.program_id(2) == 0)
    def _():
        z_ref[...] = jnp.zeros_like(z_ref)

    nc = n_chunks
    lane = _iota2((nc, L, LANES), 2)
    head0 = lane < RWKV_HEAD
    row = _iota2((2 * L, 2 * L), 0)
    col = _iota2((2 * L, 2 * L), 1)
    strict = row > col
    incl = row >= col
    eye = row == col
    tril = jnp.broadcast_to((_iota2((L, L), 0) >= _iota2((L, L), 1)).astype(BF16), (nc, L, L))
    bd = bd_ref[...]

    def stack2(x):
        return jnp.concatenate([jnp.where(head0, x, 0.0), jnp.where(head0, 0.0, x)], axis=1)

    r, lw, k, v, a, b = (ref[...].reshape(nc, L, LANES) for ref in (r_ref, lw_ref, k_ref, v_ref, a_ref, b_ref))
    cl = _dot_rhs2(tril, lw)
    cl_last = cl[:, L - 1:L, :]
    e_in = jnp.exp(cl)
    e_out = jnp.exp(-cl)
    e_end = jnp.exp(cl_last - cl)
    rt = stack2(r * e_in)
    at = stack2(a * jnp.exp(cl - lw))
    kt = stack2(k * e_out)
    bt = stack2(b * e_out)
    kc = stack2(k * e_end)
    bc = stack2(b * e_end)
    vs = stack2(v)
    aa = _dot_nt(jnp.concatenate([at, rt], axis=1), jnp.concatenate([bt, kt], axis=1))
    a_ab = jnp.where(strict, aa[:, 0:2 * L, 0:2 * L], 0.0)
    a_ak = jnp.where(strict, aa[:, 0:2 * L, 2 * L:4 * L], 0.0)
    a_rb = jnp.where(incl, aa[:, 2 * L:4 * L, 0:2 * L], 0.0)
    a_rk = jnp.where(incl, aa[:, 2 * L:4 * L, 2 * L:4 * L], 0.0)
    tm = _tri_inv(a_ab, 2 * L, L)
    a_hat = _dot(tm, at)
    u_v = _dot(tm, _dot(a_ak, vs))
    r_hat = rt + _dot(a_rb, a_hat)
    y_v = _dot(a_rb, u_v) + _dot(a_rk, vs)
    m = jnp.where(eye, jnp.exp(cl_last), 0.0) + _dot_tn(bc, a_hat)
    n = _dot_tn(bc, u_v) + _dot_tn(kc, vs)

    z = z_ref[...]
    ys = []
    for c in range(nc):
        yc2 = _dot(r_hat[c], z) + y_v[c]
        z = _dot(m[c], z) + n[c]
        ys.append(yc2[0:L] + yc2[L:2 * L])
    z_ref[...] = z
    y = jnp.concatenate(ys, axis=0)

    mean = _dot_lhs2(y, bd) * (1.0 / RWKV_HEAD)
    yc = y - mean
    var = _dot_lhs2(yc * yc, bd) * (1.0 / RWKV_HEAD)
    yn = yc * lax.rsqrt(var + RWKV_GN_EPS) * lng_ref[...] + lnb_ref[...]
    o_ref[...] = ((yn + bonus_ref[...]) * g_ref[...]).astype(o_ref.dtype)


def _rwkv_chunks(r, lw, k, v, a, b, bonus, g, ln_g, ln_b, bd64, *, batch, seq, tb):
    t, w = r.shape
    npairs = w // LANES
    nblk = seq // tb
    tok = pl.BlockSpec((tb, LANES), lambda bi, p, j: (bi * nblk + j, p))
    par = pl.BlockSpec((1, LANES), lambda bi, p, j: (0, p))
    return pl.pallas_call(
        functools.partial(_rwkv_chunk_kernel, n_chunks=tb // RWKV_CHUNK),
        grid=(batch, npairs, nblk),
        in_specs=[tok] * 8 + [par, par, pl.BlockSpec((LANES, LANES), lambda bi, p, j: (0, 0))],
        out_specs=tok,
        out_shape=jax.ShapeDtypeStruct((t, w), BF16),
        scratch_shapes=[pltpu.VMEM((LANES, LANES), F32)],
        compiler_params=_cparams(3), name="rwkv_chunks")(
            r, lw, k, v, a, b, bonus, g, ln_g.reshape(1, w), ln_b.reshape(1, w), bd64)


def _gdn_pre_kernel(p_ref, ph_ref, ba_ref, cw_ref, alog_ref, dtb_ref,
                    q_o, k_o, kb_o, vb_o, gb_o, *, tiles_per_seq):
    first = (pl.program_id(0) % tiles_per_seq) == 0
    w = GDN_WIDTH
    tm = p_ref.shape[0]
    ones = jnp.ones((LANES, LANES), BF16)

    def conv_silu(c0, c1):
        x = p_ref[:, c0:c1]
        halo = jnp.where(first, 0.0, ph_ref[:, c0:c1])
        out = x * cw_ref[3:4, c0:c1]
        for kshift in (1, 2, 3):
            out = out + _shift_rows(x, halo, kshift) * cw_ref[3 - kshift:4 - kshift, c0:c1]
        return out * _sigmoid(out)

    ba = ba_ref[...]
    beta = _sigmoid(ba)
    glog = -jnp.exp(alog_ref[...]) * _softplus(ba + dtb_ref[...])
    srow = _iota2((LANES, LANES), 0)
    for h in range(GDN_HEADS):
        sl = slice(GDN_HEAD * h, GDN_HEAD * (h + 1))
        beta_b = _dot_lhs2(beta, (srow == h).astype(BF16))
        gb_o[:, sl] = _dot_lhs2(glog, (srow == GDN_HEADS + h).astype(BF16))
        q = conv_silu(GDN_HEAD * h, GDN_HEAD * (h + 1))
        k = conv_silu(w + GDN_HEAD * h, w + GDN_HEAD * (h + 1))
        v = conv_silu(2 * w + GDN_HEAD * h, 2 * w + GDN_HEAD * (h + 1))
        qn = q * lax.rsqrt(_dot_lhs2(q * q, ones) + 1e-6) * (GDN_HEAD ** -0.5)
        kn = k * lax.rsqrt(_dot_lhs2(k * k, ones) + 1e-6)
        q_o[:, sl] = qn.astype(BF16)
        k_o[:, sl] = kn.astype(BF16)
        kb_o[:, sl] = (kn * beta_b).astype(BF16)
        vb_o[:, sl] = (v * beta_b).astype(BF16)
    del tm


def _gdn_pre(proj, seq, conv_w, a_log, dt_bias, *, tm):
    t = proj.shape[0]
    w = GDN_WIDTH
    nh = tm // SUBLANES
    pad = lambda a: jnp.zeros((1, LANES), F32).at[0, GDN_HEADS:2 * GDN_HEADS].set(a)
    full = lambda a: pl.BlockSpec(a.shape, lambda i: (0, 0))
    args = [conv_w, pad(a_log), pad(dt_bias)]
    out_spec = pl.BlockSpec((tm, w), lambda i: (i, 0))
    return pl.pallas_call(
        functools.partial(_gdn_pre_kernel, tiles_per_seq=seq // tm), grid=(t // tm,),
        in_specs=[pl.BlockSpec((tm, 3 * w), lambda i: (i, EV_QKV0 // (3 * w))),
                  pl.BlockSpec((SUBLANES, 3 * w), lambda i: (jnp.maximum(i * nh - 1, 0), EV_QKV0 // (3 * w))),
                  pl.BlockSpec((tm, LANES), lambda i: (i, EV_BA0 // LANES))] + [full(a) for a in args],
        out_specs=[out_spec] * 5,
        out_shape=[jax.ShapeDtypeStruct((t, w), BF16)] * 4 + [jax.ShapeDtypeStruct((t, w), F32)],
        compiler_params=_cparams(1), name="gdn_pre")(proj, proj, proj, *args)


def _gdn_chunk_kernel(q_ref, k_ref, kb_ref, vb_ref, gb_ref, z_ref, ng_ref, o_ref, s_ref, *, n_chunks):
    C = GDN_CHUNK

    @pl.when(pl.program_id(2) == 0)
    def _():
        s_ref[...] = jnp.zeros_like(s_ref)

    row = _iota2((C, C), 0)
    col = _iota2((C, C), 1)
    strict = row > col
    incl = row >= col
    eye = row == col
    nc = n_chunks
    tril = jnp.broadcast_to(incl.astype(BF16), (nc, C, C))

    q, k, kb, vb = (ref[...].astype(F32).reshape(nc, C, GDN_HEAD) for ref in (q_ref, k_ref, kb_ref, vb_ref))
    gc = _dot_rhs2(tril, gb_ref[...].reshape(nc, C, GDN_HEAD))
    gct = jnp.swapaxes(gc, 1, 2)
    dm = jnp.exp(jnp.where(incl, gc - gct, -1e30))
    egc = jnp.exp(gc)
    g_last = gc[:, C - 1:C, :]
    kk = _dot_nt(kb, k)
    qk = _dot_nt(q, k)
    m_mat = jnp.where(strict, kk * dm, 0.0)
    attn = qk * dm
    tinv = _tri_inv(-m_mat, C, C)
    u = _dot(tinv, vb)
    w = _dot(tinv, kb * egc)
    q_hat = q * egc - _dot(attn, w)
    o_v = _dot(attn, u)
    kd = k * jnp.exp(g_last - gc)
    mm = jnp.where(eye, jnp.exp(g_last), 0.0) - _dot_tn(kd, w)
    nn = _dot_tn(kd, u)

    s = s_ref[...]
    os_ = []
    for c in range(nc):
        os_.append(_dot(q_hat[c], s) + o_v[c])
        s = _dot(mm[c], s) + nn[c]
    s_ref[...] = s
    o = jnp.concatenate(os_, axis=0)
    o = o * lax.rsqrt(jnp.mean(o * o, axis=-1, keepdims=True) + GDN_NORM_EPS) * ng_ref[...]
    zz = z_ref[...]
    o_ref[...] = (o * (zz * _sigmoid(zz))).astype(o_ref.dtype)


def _gdn_chunks(q, k, kb, vb, gb, proj, norm_g, *, batch, seq, tb):
    t, w = q.shape
    nblk = seq // tb
    tok = pl.BlockSpec((tb, GDN_HEAD), lambda bi, h, j: (bi * nblk + j, h))
    zspec = pl.BlockSpec((tb, GDN_HEAD), lambda bi, h, j: (bi * nblk + j, EV_Z0 // GDN_HEAD + h))
    return pl.pallas_call(
        functools.partial(_gdn_chunk_kernel, n_chunks=tb // GDN_CHUNK),
        grid=(batch, GDN_HEADS, nblk),
        in_specs=[tok] * 5 + [zspec, pl.BlockSpec((1, GDN_HEAD), lambda bi, h, j: (0, 0))],
        out_specs=tok,
        out_shape=jax.ShapeDtypeStruct((t, w), BF16),
        scratch_shapes=[pltpu.VMEM((GDN_HEAD, GDN_HEAD), F32)],
        compiler_params=_cparams(3), name="gdn_chunks")(q, k, kb, vb, gb, proj, norm_g.reshape(1, GDN_HEAD))


def _head_block_ones():
    i = jnp.arange(LANES)
    return ((i[:, None] // RWKV_HEAD) == (i[None, :] // RWKV_HEAD)).astype(BF16)


def _pad_rows(a, rows):
    return jnp.zeros((rows,) + a.shape[1:], a.dtype).at[:a.shape[0]].set(a)


def _even_mixers(proj, batch, seq, prm):
    bd64 = _head_block_ones()
    mu = prm['ev_shift_mu']
    w = RWKV_WIDTH
    mu_lora = jnp.zeros((512,), F32)
    mu_lora = mu_lora.at[0:96].set(mu[3 * w:3 * w + 96]).at[128:224].set(mu[3 * w + 96:3 * w + 192])
    mu_lora = mu_lora.at[256:320].set(mu[3 * w + 192:3 * w + 256])
    pre = _rwkv_pre(proj, seq, mu[:3 * w], mu_lora, prm['ev_rwkv_w0'], _pad_rows(prm['ev_rwkv_w_up'], LANES),
                    prm['ev_rwkv_a0'], _pad_rows(prm['ev_rwkv_a_up'], LANES),
                    _pad_rows(prm['ev_rwkv_g_up'], LANES), prm['ev_rwkv_k_k'], prm['ev_rwkv_k_a'],
                    prm['ev_rwkv_r_k'].reshape(-1), bd64, tm=256)
    y_a = _rwkv_chunks(*pre, prm['ev_rwkv_ln_g'], prm['ev_rwkv_ln_b'], bd64, batch=batch, seq=seq, tb=1024)
    gpre = _gdn_pre(proj, seq, prm['ev_gdn_conv'], prm['ev_gdn_a_log'], prm['ev_gdn_dt_bias'], tm=256)
    y_b = _gdn_chunks(*gpre, proj, prm['ev_gdn_norm_g'], batch=batch, seq=seq, tb=2048)
    return y_a, y_b


def _lru_kernel(x_ref, xh_ref, gate_ref, cw_ref, cb_ref, rw_ref, rb_ref, iw_ref, ib_ref, l_ref,
                o_ref, h_ref):
    j = pl.program_id(2)
    tm = x_ref.shape[0]

    @pl.when(j == 0)
    def _():
        h_ref[...] = jnp.zeros_like(h_ref)

    x = x_ref[...]
    halo = jnp.where(j == 0, 0.0, xh_ref[...])
    xc = x * cw_ref[3:4, :] + cb_ref[...]
    for kshift in (1, 2, 3):
        xc = xc + _shift_rows(x, halo, kshift) * cw_ref[3 - kshift:4 - kshift, :]
    r = _sigmoid(_dot(xc, rw_ref[0]) + rb_ref[...])
    i = _sigmoid(_dot(xc, iw_ref[0]) + ib_ref[...])
    log_a = -LRU_C * r * _softplus(-l_ref[...])
    a = jnp.exp(log_a)
    b = jnp.sqrt(1.0 - jnp.exp(2.0 * log_a)) * (i * xc)
    row = _iota2(a.shape, 0)
    s = 1
    while s < tm:
        keep = row >= s
        a_sh = jnp.where(keep, pltpu.roll(a, s, 0), 1.0)
        b_sh = jnp.where(keep, pltpu.roll(b, s, 0), 0.0)
        b = a * b_sh + b
        a = a * a_sh
        s *= 2
    h = b + a * h_ref[0:1, :]
    h_ref[...] = jnp.broadcast_to(h[tm - 1:tm, :], h_ref.shape)
    o_ref[...] = (h * _gelu(gate_ref[...])).astype(o_ref.dtype)


def _lru(proj, conv_w, conv_b, r_w, r_b, i_w, i_b, lru_l, *, batch, seq, tm):
    t = proj.shape[0]
    w = LRU_BLOCKS * LRU_BLOCK
    nblk = seq // tm
    nh = tm // SUBLANES
    vec = pl.BlockSpec((1, LRU_BLOCK), lambda bi, g, j: (0, g))
    mat = pl.BlockSpec((1, LRU_BLOCK, LRU_BLOCK), lambda bi, g, j: (g, 0, 0))
    row = lambda a: a.reshape(1, w)
    return pl.pallas_call(
        _lru_kernel, grid=(batch, LRU_BLOCKS, nblk),
        in_specs=[pl.BlockSpec((tm, LRU_BLOCK), lambda bi, g, j: (bi * nblk + j, LRU_BLOCKS + g)),
                  pl.BlockSpec((SUBLANES, LRU_BLOCK),
                               lambda bi, g, j: (jnp.maximum((bi * nblk + j) * nh - 1, 0), LRU_BLOCKS + g)),
                  pl.BlockSpec((tm, LRU_BLOCK), lambda bi, g, j: (bi * nblk + j, g)),
                  pl.BlockSpec((4, LRU_BLOCK), lambda bi, g, j: (0, g)), vec, mat, vec, mat, vec, vec],
        out_specs=pl.BlockSpec((tm, LRU_BLOCK), lambda bi, g, j: (bi * nblk + j, g)),
        out_shape=jax.ShapeDtypeStruct((t, w), BF16),
        scratch_shapes=[pltpu.VMEM((SUBLANES, LRU_BLOCK), F32)],
        compiler_params=_cparams(3), name="rglru")(
            proj, proj, proj, conv_w, row(conv_b), r_w.astype(BF16), row(r_b), i_w.astype(BF16), row(i_b),
            row(lru_l))


def _oddeven_merge_sort_network(n):
    pairs = []

    def merge(lo, hi, r):
        step = 2 * r
        if step < hi - lo:
            merge(lo, hi, step)
            merge(lo + r, hi, step)
            pairs.extend((i, i + r) for i in range(lo + r, hi - r, step))
        else:
            pairs.append((lo, lo + r))

    def sort(lo, hi):
        if hi - lo >= 1:
            mid = lo + (hi - lo) // 2
            sort(lo, mid)
            sort(mid + 1, hi)
            merge(lo, hi, 1)

    sort(0, n - 1)
    return tuple(pairs)


_SORT16 = _oddeven_merge_sort_network(PEER_KEYS // SUBLANES)


def _peer_topk_kernel(q_ref, sk_ref, id_o, gate_o, v_ref, i_ref, s_ref):
    tm = q_ref.shape[0]
    neg = -jnp.inf
    ngrp = PEER_KEYS // SUBLANES
    sub = _iota2((SUBLANES, LANES), 0).astype(F32)
    nt = lambda a, b: lax.dot_general(a, b, (((1,), (1,)), ((), ())), preferred_element_type=F32)

    scores = []
    for half in range(2):
        ah, al = _split(sk_ref[0, half])
        bh, bl = _split(q_ref[:, PEER_KEYS * half:PEER_KEYS * (half + 1)])
        scores.append(nt(ah, bh) + nt(al, bh) + nt(ah, bl))

    for lb in range(tm // LANES):
        lanes = slice(LANES * lb, LANES * (lb + 1))
        for half in range(2):
            rows = [scores[half][SUBLANES * a:SUBLANES * (a + 1), lanes] for a in range(ngrp)]
            keys = [sub + float(SUBLANES * a) for a in range(ngrp)]
            for i, j in _SORT16:
                swap = rows[j] > rows[i]
                rows[i], rows[j] = jnp.maximum(rows[i], rows[j]), jnp.minimum(rows[i], rows[j])
                keys[i], keys[j] = jnp.where(swap, keys[j], keys[i]), jnp.where(swap, keys[i], keys[j])
            for kk in range(PEER_TOPK):
                m = jnp.max(rows[0], axis=0, keepdims=True)
                idx = jnp.min(jnp.where(rows[0] == m, keys[0], 1e9), axis=0, keepdims=True)
                v_ref[half, kk:kk + 1, lanes] = m
                i_ref[half, kk:kk + 1, lanes] = idx
                win = keys[0] == idx
                for a in range(PEER_TOPK - 1 - kk):
                    rows[a] = jnp.where(win, rows[a + 1], rows[a])
                    keys[a] = jnp.where(win, keys[a + 1], keys[a])

        v1, v2 = v_ref[0, :, lanes], v_ref[1, :, lanes]
        i1, i2 = i_ref[0, :, lanes], i_ref[1, :, lanes]
        lo = [v1[0:SUBLANES] + v2[b:b + 1] for b in range(PEER_TOPK)]
        lo_id = [i1[0:SUBLANES] * PEER_KEYS + i2[b:b + 1] for b in range(PEER_TOPK)]
        hi = v1[SUBLANES:PEER_TOPK] + v2[0:1]
        hi_id = i1[SUBLANES:PEER_TOPK] * PEER_KEYS + i2[0:1]
        col_lo, col_hi = sub, sub + float(SUBLANES)
        for kk in range(PEER_TOPK):
            m = jnp.maximum(jnp.max(lo[0], axis=0, keepdims=True), jnp.max(hi, axis=0, keepdims=True))
            col = jnp.minimum(jnp.min(jnp.where(lo[0] == m, col_lo, 1e9), axis=0, keepdims=True),
                              jnp.min(jnp.where(hi == m, col_hi, 1e9), axis=0, keepdims=True))
            win_lo, win_hi = col_lo == col, col_hi == col
            s_ref[kk:kk + 1, lanes] = m
            id_o[0, kk:kk + 1, lanes] = jnp.maximum(
                jnp.max(jnp.where(win_lo, lo_id[0], -1.0), axis=0, keepdims=True),
                jnp.max(jnp.where(win_hi, hi_id, -1.0), axis=0, keepdims=True))
            for b in range(PEER_TOPK - 1 - kk):
                lo[b] = jnp.where(win_lo, lo[b + 1], lo[b])
                lo_id[b] = jnp.where(win_lo, lo_id[b + 1], lo_id[b])
            hi = jnp.where(win_hi, neg, hi)
    top = s_ref[...]
    e = jnp.exp(top - top[0:1])
    gate_o[0] = e / jnp.sum(e, axis=0, keepdims=True)


def _peer_topk(q, subkeys, *, tm):
    t = q.shape[0]
    out = pl.BlockSpec((1, PEER_TOPK, tm), lambda i, h: (h, 0, i))
    return pl.pallas_call(
        _peer_topk_kernel, grid=(t // tm, PEER_HEADS),
        in_specs=[pl.BlockSpec((tm, 2 * PEER_KEYS), lambda i, h: (i, h)),
                  pl.BlockSpec((1, 2, PEER_KEYS, PEER_KEYS), lambda i, h: (h, 0, 0, 0))],
        out_specs=[out, out],
        out_shape=[jax.ShapeDtypeStruct((PEER_HEADS, PEER_TOPK, t), F32)] * 2,
        scratch_shapes=[pltpu.VMEM((2, PEER_TOPK, tm), F32), pltpu.VMEM((2, PEER_TOPK, tm), F32),
                        pltpu.VMEM((PEER_TOPK, tm), F32)],
        compiler_params=_cparams(2), name="peer_topk")(q, subkeys)


GATE_TILE = 64
GATE_STRIDE = GATE_TILE + SUBLANES


def _peer_gates_kernel(i1_ref, i2_ref, g_ref, o_ref, s_ref):
    ciota = _iota2((PEER_KEYS, LANES), 0).astype(F32)

    def body(t, carry):
        i1 = i1_ref[pl.ds(t, 1), :]
        i2 = i2_ref[pl.ds(t, 1), :]
        g = g_ref[pl.ds(t, 1), :]
        a = jnp.where(ciota == i1, g, 0.0)
        bt = jnp.where(ciota == i2, 1.0, 0.0)
        s_ref[pl.ds(t, PEER_KEYS, stride=GATE_STRIDE), :] = _dot_nt(a, bt)
        return carry

    lax.fori_loop(0, GATE_TILE, body, 0, unroll=16)
    for c in range(PEER_KEYS):
        o_ref[c] = s_ref[GATE_STRIDE * c:GATE_STRIDE * c + GATE_TILE, :].astype(o_ref.dtype)


def _peer_gates(i1, i2, gates):
    t = i1.shape[0]
    tok = pl.BlockSpec((GATE_TILE, LANES), lambda i: (i, 0))
    return pl.pallas_call(
        _peer_gates_kernel, grid=(t // GATE_TILE,),
        in_specs=[tok, tok, tok],
        out_specs=pl.BlockSpec((PEER_KEYS, GATE_TILE, PEER_KEYS), lambda i: (0, i, 0)),
        out_shape=jax.ShapeDtypeStruct((PEER_KEYS, t, PEER_KEYS), BF16),
        scratch_shapes=[pltpu.VMEM((PEER_KEYS * GATE_STRIDE, PEER_KEYS), F32)],
        compiler_params=_cparams(1), name="peer_gates")(i1, i2, gates)


PEER_STEP = 8 * PEER_KEYS

def _peer_ffn_kernel(sc_ref, x_ref, xr_ref, ut_ref, v_ref, g_ref, o_ref):
    @pl.when(pl.program_id(1) == 0)
    def _():
        o_ref[...] = jnp.zeros_like(o_ref)

    nslab = PEER_STEP // PEER_KEYS
    xr = xr_ref[...] * sc_ref[0]
    h = jnp.dot(x_ref[...], ut_ref[...], preferred_element_type=F32)
    wgt = jnp.concatenate(
        [_gelu(h[:, PEER_KEYS * c:PEER_KEYS * (c + 1)] * xr) * g_ref[c].astype(F32) for c in range(nslab)], axis=1)
    amax = jnp.maximum(jnp.max(jnp.abs(wgt), axis=-1, keepdims=True), F8_TINY)
    w8 = (wgt * (F8_RANGE / amax)).astype(F8)
    part = jnp.dot(w8, v_ref[...], preferred_element_type=F32)
    o_ref[...] += part * (amax * (sc_ref[1] / F8_RANGE))


def _peer_ffn(x8, xr, ut8, v8, descale, gmat, *, tm):
    t, d = x8.shape
    ne = ut8.shape[1]
    return pl.pallas_call(
        _peer_ffn_kernel, grid=(t // tm, ne // PEER_STEP),
        in_specs=[pl.BlockSpec(memory_space=pltpu.SMEM),
                  pl.BlockSpec((tm, d), lambda i, j: (i, 0)),
                  pl.BlockSpec((tm, LANES), lambda i, j: (i, 0)),
                  pl.BlockSpec((d, PEER_STEP), lambda i, j: (0, j)),
                  pl.BlockSpec((PEER_STEP, d), lambda i, j: (j, 0)),
                  pl.BlockSpec((PEER_STEP // PEER_KEYS, tm, PEER_KEYS), lambda i, j: (j, i, 0))],
        out_specs=pl.BlockSpec((tm, d), lambda i, j: (i, 0)),
        out_shape=jax.ShapeDtypeStruct((t, d), F32),
        compiler_params=_cparams(2), name="peer_ffn")(descale, x8, xr, ut8, v8, gmat)


def _ple_kernel(x_ref, f_ref, p_ref, g_ref, b_ref, wg_ref, bg_ref, wp_ref, o_ref):
    x2 = _layer_norm(DN_ALPHA * x_ref[...] + f_ref[...], g_ref[...], b_ref[...])
    gate = _sigmoid(_dot(x2, wg_ref[...]) + bg_ref[...])
    o_ref[...] = x2 + gate * _dot(p_ref[...], wp_ref[...])


def _ple(x, ffn, p, ln_g, ln_b, w_gate, b_gate, w_proj, *, tm):
    t, d = x.shape
    pd = p.shape[1]
    tok = pl.BlockSpec((tm, d), lambda i: (i, 0))
    vec = pl.BlockSpec((1, d), lambda i: (0, 0))
    return pl.pallas_call(
        _ple_kernel, grid=(t // tm,),
        in_specs=[tok, tok, pl.BlockSpec((tm, pd), lambda i: (i, 0)), vec, vec,
                  pl.BlockSpec((d, d), lambda i: (0, 0)), vec, pl.BlockSpec((pd, d), lambda i: (0, 0))],
        out_specs=tok, out_shape=jax.ShapeDtypeStruct((t, d), F32),
        compiler_params=_cparams(1), name="ple")(
            x, ffn, p, ln_g.reshape(1, d), ln_b.reshape(1, d), w_gate.astype(BF16), b_gate.reshape(1, d),
            w_proj.astype(BF16))


def _fp8_table(tab):
    amax = jnp.maximum(jnp.max(jnp.abs(tab)), F8_TINY)
    return (tab * (F8_RANGE / amax)).astype(F8), amax * (1.0 / F8_RANGE)


def _peer_block(xn_b, x8, xr, w_q, subkeys, u_tab, v_tab):
    t = xn_b.shape[0]
    q = _matmul(xn_b, w_q.astype(BF16), tm=1024, tn=512)
    ids, gates = _peer_topk(q, subkeys, tm=1024)
    to_tok = lambda a: a.reshape(PEER_HEADS * PEER_TOPK, t).T
    ids = to_tok(ids)
    i1 = jnp.floor(ids * (1.0 / PEER_KEYS))
    i2 = ids - PEER_KEYS * i1
    gmat = _peer_gates(i1, i2, to_tok(gates))
    ut8, du = _fp8_table(u_tab.T)
    v8, dv = _fp8_table(v_tab)
    return _peer_ffn(x8, xr, ut8, v8, jnp.stack([du, dv]), gmat, tm=1024)


def _even_in_weight(w_in):
    w = RWKV_WIDTH
    rk = 3 * w
    lora = lambda c0, n: jnp.pad(w_in[:, c0:c0 + n], ((0, 0), (0, LANES - n)))
    g0 = rk + 256
    cols = [w_in[:, g0:g0 + 3 * GDN_WIDTH],
            w_in[:, 0:rk],
            w_in[:, g0 + 3 * GDN_WIDTH:g0 + 4 * GDN_WIDTH],
            lora(rk, 96), lora(rk + 96, 96), lora(rk + 192, 64),
            lora(g0 + 4 * GDN_WIDTH, 2 * GDN_HEADS)]
    return jnp.concatenate(cols, axis=1).astype(BF16)


def kernel(x, p, ev_w_in, ev_shift_mu, ev_rwkv_w0, ev_rwkv_w_up, ev_rwkv_a0, ev_rwkv_a_up, ev_rwkv_g_up, ev_rwkv_k_k, ev_rwkv_k_a, ev_rwkv_r_k, ev_rwkv_ln_g, ev_rwkv_ln_b, ev_gdn_conv, ev_gdn_a_log, ev_gdn_dt_bias, ev_gdn_norm_g, ev_w_out, od_w_in, od_conv_w, od_conv_b, od_r_w, od_r_b, od_i_w, od_i_b, od_lru_L, od_w_out, ln_mix_g, ln_mix_b, peer_w_q, peer_subkeys, peer_u, peer_v, ln_ffn_g, ln_ffn_b, ple_w_proj, ple_w_gate, ple_b_gate):
    batch, seq, d = x.shape
    t = batch * seq
    xt = x.reshape(t, d)
    ev = dict(ev_shift_mu=ev_shift_mu, ev_rwkv_w0=ev_rwkv_w0, ev_rwkv_w_up=ev_rwkv_w_up, ev_rwkv_a0=ev_rwkv_a0,
              ev_rwkv_a_up=ev_rwkv_a_up, ev_rwkv_g_up=ev_rwkv_g_up, ev_rwkv_k_k=ev_rwkv_k_k,
              ev_rwkv_k_a=ev_rwkv_k_a, ev_rwkv_r_k=ev_rwkv_r_k, ev_rwkv_ln_g=ev_rwkv_ln_g,
              ev_rwkv_ln_b=ev_rwkv_ln_b, ev_gdn_conv=ev_gdn_conv, ev_gdn_a_log=ev_gdn_a_log,
              ev_gdn_dt_bias=ev_gdn_dt_bias, ev_gdn_norm_g=ev_gdn_norm_g)
    for layer in range(p.shape[0]):
        j = layer // 2
        if layer % 2 == 0:
            proj = _matmul(xt, _even_in_weight(ev_w_in[j]), tm=1024, tn=512)
            y_a, y_b = _even_mixers(proj, batch, seq, {k: v[j] for k, v in ev.items()})
            w_out = ev_w_out[j].astype(BF16)
            xn, xn_b, x8, xr = _proj_ln([y_a, y_b], [w_out[:RWKV_WIDTH], w_out[RWKV_WIDTH:]], xt,
                                        ln_mix_g[layer], ln_mix_b[layer], tm=256)
        else:
            proj = _matmul(xt, od_w_in[j].astype(BF16), tm=1024, tn=512)
            y = _lru(proj, od_conv_w[j], od_conv_b[j], od_r_w[j], od_r_b[j], od_i_w[j], od_i_b[j],
                     od_lru_L[j], batch=batch, seq=seq, tm=512)
            xn, xn_b, x8, xr = _proj_ln([y], [od_w_out[j].astype(BF16)], xt, ln_mix_g[layer], ln_mix_b[layer],
                                        tm=256)
        ffn = _peer_block(xn_b, x8, xr, peer_w_q[layer], peer_subkeys[layer], peer_u[layer], peer_v[layer])
        xt = _ple(xn, ffn, p[layer].reshape(t, -1), ln_ffn_g[layer], ln_ffn_b[layer], ple_w_gate[layer],
                  ple_b_gate[layer], ple_w_proj[layer], tm=256)
    return xt.reshape(batch, seq, d)
```

```python
import functools

import jax
import jax.numpy as jnp
from jax import lax
from jax.experimental import pallas as pl
from jax.experimental.pallas import tpu as pltpu

F32 = jnp.float32
BF16 = jnp.bfloat16
F8 = jnp.float8_e4m3fn
F8_RANGE = 224.0
F8_TINY = 1e-30

D_MODEL = 2048
DEPTH = 2
DN_ALPHA = (2.0 * DEPTH) ** 0.25
LN_EPS = 1e-5

RWKV_WIDTH = 1024
RWKV_HEAD = 64
RWKV_GN_EPS = 64e-5
RWKV_CHUNK = 64
GDN_WIDTH = 1024
GDN_HEAD = 128
GDN_HEADS = 8
GDN_CHUNK = 128
GDN_NORM_EPS = 1e-6
LRU_BLOCKS = 8
LRU_BLOCK = 256
LRU_C = 8.0
PEER_HEADS = 8
PEER_KEYS = 128
PEER_TOPK = 16

LANES = 128
SUBLANES = 8
VMEM_LIMIT = 56 * 1024 * 1024

EV_QKV0 = 0
EV_RKV0 = 3072
EV_Z0 = 6144
EV_LORA0 = 7168
EV_BA0 = 7552
EV_COLS = 7680


def _cparams(n_axes):
    return pltpu.CompilerParams(dimension_semantics=("arbitrary",) * n_axes,
                                vmem_limit_bytes=VMEM_LIMIT)


def _dot(a, b):
    a, b = a.astype(BF16), b.astype(BF16)
    if a.ndim == 3:
        return lax.dot_general(a, b, (((2,), (1,)), ((0,), (0,))), preferred_element_type=F32)
    return jnp.dot(a, b, preferred_element_type=F32)


def _dot_nt(a, b):
    a, b = a.astype(BF16), b.astype(BF16)
    if a.ndim == 3:
        return lax.dot_general(a, b, (((2,), (2,)), ((0,), (0,))), preferred_element_type=F32)
    return lax.dot_general(a, b, (((1,), (1,)), ((), ())), preferred_element_type=F32)


def _dot_tn(a, b):
    return _dot(jnp.swapaxes(a, -1, -2), b)


def _split(x):
    hi = x.astype(BF16)
    lo = (x - hi.astype(F32)).astype(BF16)
    return hi, lo


def _dot_lhs2(a, b):
    hi, lo = _split(a)
    return _dot(hi, b) + _dot(lo, b)


def _dot_rhs2(a, b):
    hi, lo = _split(b)
    return _dot(a, hi) + _dot(a, lo)


def _dot3(a, b):
    ah, al = _split(a)
    bh, bl = _split(b)
    return _dot(ah, bh) + _dot(al, bh) + _dot(ah, bl)


def _sigmoid(x):
    return 1.0 / (1.0 + jnp.exp(-x))


def _softplus(x):
    return jnp.maximum(x, 0.0) + jnp.log(1.0 + jnp.exp(-jnp.abs(x)))


def _gelu(x):
    return 0.5 * x * (1.0 + jnp.tanh(0.7978845608028654 * (x + 0.044715 * (x * x * x))))


def _iota2(shape, axis):
    return lax.broadcasted_iota(jnp.int32, shape, axis)


def _layer_norm(x, g, b):
    mu = jnp.mean(x, axis=-1, keepdims=True)
    xc = x - mu
    var = jnp.mean(xc * xc, axis=-1, keepdims=True)
    return xc * lax.rsqrt(var + LN_EPS) * g + b


def _shift_rows(x, halo, k):
    rolled = pltpu.roll(x, k, 0)
    hr = pltpu.roll(halo, k, 0)
    row = _iota2(halo.shape, 0)
    top = jnp.where(row < k, hr, rolled[0:SUBLANES])
    return jnp.concatenate([top, rolled[SUBLANES:]], axis=0)


def _tri_inv(x, n, top):
    row = _iota2((n, n), 0)
    col = _iota2((n, n), 1)
    eye = (row == col).astype(F32)
    x0 = jnp.where((row >> 3) == (col >> 3), x, 0.0)
    x2 = _dot(x0, x0)
    x4 = _dot(x2, x2)
    t = eye + x0
    t = t + _dot(t, x2)
    t = t + _dot(t, x4)
    s = 8
    while s < top:
        sh = s.bit_length() - 1
        off = ((row >> (sh + 1)) == (col >> (sh + 1))) & ((row >> sh) != (col >> sh))
        xo = jnp.where(off, x, 0.0)
        t = t + _dot(_dot(t, xo), t)
        s *= 2
    return t


def _mm_kernel(x_ref, w_ref, o_ref):
    o_ref[...] = jnp.dot(x_ref[...].astype(BF16), w_ref[...],
                         preferred_element_type=F32).astype(o_ref.dtype)


def _matmul(x, w, *, tm, tn, out_dtype=F32):
    m, k = x.shape
    n = w.shape[1]
    return pl.pallas_call(
        _mm_kernel, grid=(m // tm, n // tn),
        in_specs=[pl.BlockSpec((tm, k), lambda i, j: (i, 0)),
                  pl.BlockSpec((k, tn), lambda i, j: (0, j))],
        out_specs=pl.BlockSpec((tm, tn), lambda i, j: (i, j)),
        out_shape=jax.ShapeDtypeStruct((m, n), out_dtype),
        compiler_params=_cparams(2), name="matmul")(x, w)


def _proj_ln_kernel(*refs, n_in):
    ys = refs[:n_in]
    ws = refs[n_in:2 * n_in]
    x_ref, g_ref, b_ref, o_ref, ob_ref, o8_ref, r8_ref = refs[2 * n_in:]
    mix = _dot(ys[0][...], ws[0][...])
    for y_ref, w_ref in zip(ys[1:], ws[1:]):
        mix = mix + _dot(y_ref[...], w_ref[...])
    xn = _layer_norm(DN_ALPHA * x_ref[...] + mix, g_ref[...], b_ref[...])
    o_ref[...] = xn
    ob_ref[...] = xn.astype(BF16)
    amax = jnp.maximum(jnp.max(jnp.abs(xn), axis=-1, keepdims=True), F8_TINY)
    o8_ref[...] = (xn * (F8_RANGE / amax)).astype(F8)
    r8_ref[...] = jnp.broadcast_to(amax * (1.0 / F8_RANGE), r8_ref.shape)


def _proj_ln(ys, ws, x, g, b, *, tm):
    t, d = x.shape
    n_in = len(ys)
    in_specs = ([pl.BlockSpec((tm, y.shape[1]), lambda i: (i, 0)) for y in ys]
                + [pl.BlockSpec(w.shape, lambda i: (0, 0)) for w in ws]
                + [pl.BlockSpec((tm, d), lambda i: (i, 0)),
                   pl.BlockSpec((1, d), lambda i: (0, 0)),
                   pl.BlockSpec((1, d), lambda i: (0, 0))])
    return pl.pallas_call(
        functools.partial(_proj_ln_kernel, n_in=n_in), grid=(t // tm,),
        in_specs=in_specs,
        out_specs=[pl.BlockSpec((tm, d), lambda i: (i, 0))] * 3 + [pl.BlockSpec((tm, LANES), lambda i: (i, 0))],
        out_shape=[jax.ShapeDtypeStruct((t, d), F32), jax.ShapeDtypeStruct((t, d), BF16),
                   jax.ShapeDtypeStruct((t, d), F8), jax.ShapeDtypeStruct((t, LANES), F32)],
        compiler_params=_cparams(1), name="proj_ln")(*ys, *ws, x, g.reshape(1, d), b.reshape(1, d))


def _rwkv_pre_kernel(p_ref, ph_ref, l_ref, lh_ref, mu_ref, mul_ref, w0_ref, wup_ref, a0_ref, aup_ref,
                     gup_ref, kk_ref, ka_ref, rk_ref, bd_ref,
                     r_o, lw_o, k_o, v_o, a_o, b_o, bonus_o, g_o, *, tiles_per_seq):
    first = (pl.program_id(0) % tiles_per_seq) == 0
    w = RWKV_WIDTH

    def mixed(ref, href, m_ref, c0, c1):
        x = ref[:, c0:c1]
        hrow = jnp.where(first, 0.0, href[SUBLANES - 1:SUBLANES, c0:c1])
        prev = pltpu.roll(x, 1, 0)
        prev = jnp.where(_iota2(x.shape, 0) == 0, hrow, prev)
        return x + (prev - x) * m_ref[:, c0:c1]

    def segsum(x):
        bd = bd_ref[...]
        return jnp.concatenate(
            [_dot_lhs2(x[:, LANES * p:LANES * (p + 1)], bd) for p in range(w // LANES)], axis=1)

    r = mixed(p_ref, ph_ref, mu_ref, 0, w)
    k = mixed(p_ref, ph_ref, mu_ref, w, 2 * w)
    v = mixed(p_ref, ph_ref, mu_ref, 2 * w, 3 * w)
    wd = mixed(l_ref, lh_ref, mul_ref, 0, LANES)
    ad = mixed(l_ref, lh_ref, mul_ref, LANES, 2 * LANES)
    gd = mixed(l_ref, lh_ref, mul_ref, 2 * LANES, 3 * LANES)

    w_log = -_softplus(-(w0_ref[...] + _dot3(jnp.tanh(wd), wup_ref[...]))) - 0.5
    lw_o[...] = -jnp.exp(w_log)
    a = _sigmoid(a0_ref[...] + _dot3(ad, aup_ref[...]))
    g_o[...] = _dot3(_sigmoid(gd), gup_ref[...])
    kkr = k * kk_ref[...]
    kk = kkr * lax.rsqrt(segsum(kkr * kkr) + 1e-6)
    k2 = k * (1.0 + (a - 1.0) * ka_ref[...])
    bonus_o[...] = segsum(r * k2 * rk_ref[...]) * v
    r_o[...] = r
    k_o[...] = k2
    v_o[...] = v
    a_o[...] = -kk
    b_o[...] = kk * a


def _rwkv_pre(proj, seq, mu_rkv, mu_lora, w0, w_up, a0, a_up, g_up, k_k, k_a, r_k, bd64, *, tm):
    t = proj.shape[0]
    w = RWKV_WIDTH
    nh = tm // SUBLANES
    row = lambda a: a.reshape(1, -1)
    full = lambda a: pl.BlockSpec(a.shape, lambda i: (0, 0))
    args = [row(mu_rkv), row(mu_lora), row(w0), w_up, row(a0), a_up, g_up, row(k_k), row(k_a), row(r_k), bd64]
    in_specs = [pl.BlockSpec((tm, 3 * w), lambda i: (i, EV_RKV0 // (3 * w))),
                pl.BlockSpec((SUBLANES, 3 * w), lambda i: (jnp.maximum(i * nh - 1, 0), EV_RKV0 // (3 * w))),
                pl.BlockSpec((tm, 512), lambda i: (i, EV_LORA0 // 512)),
                pl.BlockSpec((SUBLANES, 512), lambda i: (jnp.maximum(i * nh - 1, 0), EV_LORA0 // 512))]
    in_specs += [full(a) for a in args]
    out_spec = pl.BlockSpec((tm, w), lambda i: (i, 0))
    return pl.pallas_call(
        functools.partial(_rwkv_pre_kernel, tiles_per_seq=seq // tm), grid=(t // tm,),
        in_specs=in_specs, out_specs=[out_spec] * 8,
        out_shape=[jax.ShapeDtypeStruct((t, w), F32)] * 8,
        compiler_params=_cparams(1), name="rwkv_pre")(proj, proj, proj, proj, *args)


def _rwkv_chunk_kernel(r_ref, lw_ref, k_ref, v_ref, a_ref, b_ref, bonus_ref, g_ref, lng_ref, lnb_ref,
                       bd_ref, o_ref, z_ref, *, n_chunks):
    L = RWKV_CHUNK

    @pl.when(pl.program_id(2) == 0)
    def _():
        z_ref[...] = jnp.zeros_like(z_ref)

    nc = n_chunks
    lane = _iota2((nc, L, LANES), 2)
    head0 = lane < RWKV_HEAD
    row = _iota2((2 * L, 2 * L), 0)
    col = _iota2((2 * L, 2 * L), 1)
    strict = row > col
    incl = row >= col
    eye = row == col
    tril = jnp.broadcast_to((_iota2((L, L), 0) >= _iota2((L, L), 1)).astype(BF16), (nc, L, L))
    bd = bd_ref[...]

    def stack2(x):
        return jnp.concatenate([jnp.where(head0, x, 0.0), jnp.where(head0, 0.0, x)], axis=1)

    r, lw, k, v, a, b = (ref[...].reshape(nc, L, LANES) for ref in (r_ref, lw_ref, k_ref, v_ref, a_ref, b_ref))
    cl = _dot_rhs2(tril, lw)
    cl_last = cl[:, L - 1:L, :]
    e_in = jnp.exp(cl)
    e_out = jnp.exp(-cl)
    e_end = jnp.exp(cl_last - cl)
    rt = stack2(r * e_in)
    at = stack2(a * jnp.exp(cl - lw))
    kt = stack2(k * e_out)
    bt = stack2(b * e_out)
    kc = stack2(k * e_end)
    bc = stack2(b * e_end)
    vs = stack2(v)
    aa = _dot_nt(jnp.concatenate([at, rt], axis=1), jnp.concatenate([bt, kt], axis=1))
    a_ab = jnp.where(strict, aa[:, 0:2 * L, 0:2 * L], 0.0)
    a_ak = jnp.where(strict, aa[:, 0:2 * L, 2 * L:4 * L], 0.0)
    a_rb = jnp.where(incl, aa[:, 2 * L:4 * L, 0:2 * L], 0.0)
    a_rk = jnp.where(incl, aa[:, 2 * L:4 * L, 2 * L:4 * L], 0.0)
    tm = _tri_inv(a_ab, 2 * L, L)
    hat = _dot(tm, jnp.concatenate([at, _dot(a_ak, vs)], axis=2))
    w2 = jnp.concatenate([hat, jnp.concatenate([jnp.zeros_like(vs), vs], axis=2)], axis=1)
    ry = _dot(jnp.concatenate([a_rb, a_rk], axis=2), w2)
    r_hat = rt + ry[:, :, 0:2 * L]
    y_v = ry[:, :, 2 * L:4 * L]
    mn = _dot_tn(jnp.concatenate([bc, kc], axis=1), w2)
    m = jnp.where(eye, jnp.exp(cl_last), 0.0) + mn[:, :, 0:2 * L]
    n = mn[:, :, 2 * L:4 * L]

    z = z_ref[...]
    ys = []
    for c in range(nc):
        yc2 = _dot(r_hat[c], z) + y_v[c]
        z = _dot(m[c], z) + n[c]
        ys.append(yc2[0:L] + yc2[L:2 * L])
    z_ref[...] = z
    y = jnp.concatenate(ys, axis=0)

    mean = _dot_lhs2(y, bd) * (1.0 / RWKV_HEAD)
    yc = y - mean
    var = _dot_lhs2(yc * yc, bd) * (1.0 / RWKV_HEAD)
    yn = yc * lax.rsqrt(var + RWKV_GN_EPS) * lng_ref[...] + lnb_ref[...]
    o_ref[...] = ((yn + bonus_ref[...]) * g_ref[...]).astype(o_ref.dtype)


def _rwkv_chunks(r, lw, k, v, a, b, bonus, g, ln_g, ln_b, bd64, *, batch, seq, tb):
    t, w = r.shape
    npairs = w // LANES
    nblk = seq // tb
    tok = pl.BlockSpec((tb, LANES), lambda bi, p, j: (bi * nblk + j, p))
    par = pl.BlockSpec((1, LANES), lambda bi, p, j: (0, p))
    return pl.pallas_call(
        functools.partial(_rwkv_chunk_kernel, n_chunks=tb // RWKV_CHUNK),
        grid=(batch, npairs, nblk),
        in_specs=[tok] * 8 + [par, par, pl.BlockSpec((LANES, LANES), lambda bi, p, j: (0, 0))],
        out_specs=tok,
        out_shape=jax.ShapeDtypeStruct((t, w), BF16),
        scratch_shapes=[pltpu.VMEM((LANES, LANES), F32)],
        compiler_params=_cparams(3), name="rwkv_chunks")(
            r, lw, k, v, a, b, bonus, g, ln_g.reshape(1, w), ln_b.reshape(1, w), bd64)


def _gdn_pre_kernel(p_ref, ph_ref, ba_ref, cw_ref, alog_ref, dtb_ref,
                    q_o, k_o, kb_o, vb_o, gb_o, *, tiles_per_seq):
    first = (pl.program_id(0) % tiles_per_seq) == 0
    w = GDN_WIDTH
    tm = p_ref.shape[0]
    ones = jnp.ones((LANES, LANES), BF16)

    def conv_silu(c0, c1):
        x = p_ref[:, c0:c1]
        halo = jnp.where(first, 0.0, ph_ref[:, c0:c1])
        out = x * cw_ref[3:4, c0:c1]
        for kshift in (1, 2, 3):
            out = out + _shift_rows(x, halo, kshift) * cw_ref[3 - kshift:4 - kshift, c0:c1]
        return out * _sigmoid(out)

    ba = ba_ref[...]
    beta = _sigmoid(ba)
    glog = -jnp.exp(alog_ref[...]) * _softplus(ba + dtb_ref[...])
    srow = _iota2((LANES, LANES), 0)
    for h in range(GDN_HEADS):
        sl = slice(GDN_HEAD * h, GDN_HEAD * (h + 1))
        beta_b = _dot_lhs2(beta, (srow == h).astype(BF16))
        gb_o[:, sl] = _dot_lhs2(glog, (srow == GDN_HEADS + h).astype(BF16))
        q = conv_silu(GDN_HEAD * h, GDN_HEAD * (h + 1))
        k = conv_silu(w + GDN_HEAD * h, w + GDN_HEAD * (h + 1))
        v = conv_silu(2 * w + GDN_HEAD * h, 2 * w + GDN_HEAD * (h + 1))
        qn = q * lax.rsqrt(_dot_lhs2(q * q, ones) + 1e-6) * (GDN_HEAD ** -0.5)
        kn = k * lax.rsqrt(_dot_lhs2(k * k, ones) + 1e-6)
        q_o[:, sl] = qn.astype(BF16)
        k_o[:, sl] = kn.astype(BF16)
        kb_o[:, sl] = (kn * beta_b).astype(BF16)
        vb_o[:, sl] = (v * beta_b).astype(BF16)
    del tm


def _gdn_pre(proj, seq, conv_w, a_log, dt_bias, *, tm):
    t = proj.shape[0]
    w = GDN_WIDTH
    nh = tm // SUBLANES
    pad = lambda a: jnp.zeros((1, LANES), F32).at[0, GDN_HEADS:2 * GDN_HEADS].set(a)
    full = lambda a: pl.BlockSpec(a.shape, lambda i: (0, 0))
    args = [conv_w, pad(a_log), pad(dt_bias)]
    out_spec = pl.BlockSpec((tm, w), lambda i: (i, 0))
    return pl.pallas_call(
        functools.partial(_gdn_pre_kernel, tiles_per_seq=seq // tm), grid=(t // tm,),
        in_specs=[pl.BlockSpec((tm, 3 * w), lambda i: (i, EV_QKV0 // (3 * w))),
                  pl.BlockSpec((SUBLANES, 3 * w), lambda i: (jnp.maximum(i * nh - 1, 0), EV_QKV0 // (3 * w))),
                  pl.BlockSpec((tm, LANES), lambda i: (i, EV_BA0 // LANES))] + [full(a) for a in args],
        out_specs=[out_spec] * 5,
        out_shape=[jax.ShapeDtypeStruct((t, w), BF16)] * 4 + [jax.ShapeDtypeStruct((t, w), F32)],
        compiler_params=_cparams(1), name="gdn_pre")(proj, proj, proj, *args)


def _gdn_chunk_kernel(q_ref, k_ref, kb_ref, vb_ref, gb_ref, z_ref, ng_ref, o_ref, s_ref, *, n_chunks):
    C = GDN_CHUNK

    @pl.when(pl.program_id(2) == 0)
    def _():
        s_ref[...] = jnp.zeros_like(s_ref)

    row = _iota2((C, C), 0)
    col = _iota2((C, C), 1)
    strict = row > col
    incl = row >= col
    eye = row == col
    nc = n_chunks
    tril = jnp.broadcast_to(incl.astype(BF16), (nc, C, C))

    q, k, kb, vb = (ref[...].astype(F32).reshape(nc, C, GDN_HEAD) for ref in (q_ref, k_ref, kb_ref, vb_ref))
    gc = _dot_rhs2(tril, gb_ref[...].reshape(nc, C, GDN_HEAD))
    gct = jnp.swapaxes(gc, 1, 2)
    dm = jnp.exp(jnp.where(incl, gc - gct, -1e30))
    egc = jnp.exp(gc)
    g_last = gc[:, C - 1:C, :]
    kk = _dot_nt(kb, k)
    qk = _dot_nt(q, k)
    m_mat = jnp.where(strict, kk * dm, 0.0)
    attn = qk * dm
    tinv = _tri_inv(-m_mat, C, C)
    uw = _dot(tinv, jnp.concatenate([vb, kb * egc], axis=2))
    au = _dot(attn, uw)
    o_v = au[:, :, 0:C]
    q_hat = q * egc - au[:, :, C:2 * C]
    kd = k * jnp.exp(g_last - gc)
    ku = _dot_tn(kd, uw)
    nn = ku[:, :, 0:C]
    mm = jnp.where(eye, jnp.exp(g_last), 0.0) - ku[:, :, C:2 * C]

    s = s_ref[...]
    os_ = []
    for c in range(nc):
        os_.append(_dot(q_hat[c], s) + o_v[c])
        s = _dot(mm[c], s) + nn[c]
    s_ref[...] = s
    o = jnp.concatenate(os_, axis=0)
    o = o * lax.rsqrt(jnp.mean(o * o, axis=-1, keepdims=True) + GDN_NORM_EPS) * ng_ref[...]
    zz = z_ref[...]
    o_ref[...] = (o * (zz * _sigmoid(zz))).astype(o_ref.dtype)


def _gdn_chunks(q, k, kb, vb, gb, proj, norm_g, *, batch, seq, tb):
    t, w = q.shape
    nblk = seq // tb
    tok = pl.BlockSpec((tb, GDN_HEAD), lambda bi, h, j: (bi * nblk + j, h))
    zspec = pl.BlockSpec((tb, GDN_HEAD), lambda bi, h, j: (bi * nblk + j, EV_Z0 // GDN_HEAD + h))
    return pl.pallas_call(
        functools.partial(_gdn_chunk_kernel, n_chunks=tb // GDN_CHUNK),
        grid=(batch, GDN_HEADS, nblk),
        in_specs=[tok] * 5 + [zspec, pl.BlockSpec((1, GDN_HEAD), lambda bi, h, j: (0, 0))],
        out_specs=tok,
        out_shape=jax.ShapeDtypeStruct((t, w), BF16),
        scratch_shapes=[pltpu.VMEM((GDN_HEAD, GDN_HEAD), F32)],
        compiler_params=_cparams(3), name="gdn_chunks")(q, k, kb, vb, gb, proj, norm_g.reshape(1, GDN_HEAD))


def _head_block_ones():
    i = jnp.arange(LANES)
    return ((i[:, None] // RWKV_HEAD) == (i[None, :] // RWKV_HEAD)).astype(BF16)


def _pad_rows(a, rows):
    return jnp.zeros((rows,) + a.shape[1:], a.dtype).at[:a.shape[0]].set(a)


def _even_mixers(proj, batch, seq, prm):
    bd64 = _head_block_ones()
    mu = prm['ev_shift_mu']
    w = RWKV_WIDTH
    mu_lora = jnp.zeros((512,), F32)
    mu_lora = mu_lora.at[0:96].set(mu[3 * w:3 * w + 96]).at[128:224].set(mu[3 * w + 96:3 * w + 192])
    mu_lora = mu_lora.at[256:320].set(mu[3 * w + 192:3 * w + 256])
    pre = _rwkv_pre(proj, seq, mu[:3 * w], mu_lora, prm['ev_rwkv_w0'], _pad_rows(prm['ev_rwkv_w_up'], LANES),
                    prm['ev_rwkv_a0'], _pad_rows(prm['ev_rwkv_a_up'], LANES),
                    _pad_rows(prm['ev_rwkv_g_up'], LANES), prm['ev_rwkv_k_k'], prm['ev_rwkv_k_a'],
                    prm['ev_rwkv_r_k'].reshape(-1), bd64, tm=256)
    y_a = _rwkv_chunks(*pre, prm['ev_rwkv_ln_g'], prm['ev_rwkv_ln_b'], bd64, batch=batch, seq=seq, tb=1024)
    gpre = _gdn_pre(proj, seq, prm['ev_gdn_conv'], prm['ev_gdn_a_log'], prm['ev_gdn_dt_bias'], tm=256)
    y_b = _gdn_chunks(*gpre, proj, prm['ev_gdn_norm_g'], batch=batch, seq=seq, tb=2048)
    return y_a, y_b


def _lru_kernel(x_ref, xh_ref, gate_ref, cw_ref, cb_ref, rw_ref, rb_ref, iw_ref, ib_ref, l_ref,
                o_ref, h_ref):
    j = pl.program_id(2)
    tm = x_ref.shape[0]

    @pl.when(j == 0)
    def _():
        h_ref[...] = jnp.zeros_like(h_ref)

    x = x_ref[...]
    halo = jnp.where(j == 0, 0.0, xh_ref[...])
    xc = x * cw_ref[3:4, :] + cb_ref[...]
    for kshift in (1, 2, 3):
        xc = xc + _shift_rows(x, halo, kshift) * cw_ref[3 - kshift:4 - kshift, :]
    r = _sigmoid(_dot(xc, rw_ref[0]) + rb_ref[...])
    i = _sigmoid(_dot(xc, iw_ref[0]) + ib_ref[...])
    log_a = -LRU_C * r * _softplus(-l_ref[...])
    a = jnp.exp(log_a)
    b = jnp.sqrt(1.0 - jnp.exp(2.0 * log_a)) * (i * xc)
    row = _iota2(a.shape, 0)
    s = 1
    while s < tm:
        keep = row >= s
        a_sh = jnp.where(keep, pltpu.roll(a, s, 0), 1.0)
        b_sh = jnp.where(keep, pltpu.roll(b, s, 0), 0.0)
        b = a * b_sh + b
        a = a * a_sh
        s *= 2
    h = b + a * h_ref[0:1, :]
    h_ref[...] = jnp.broadcast_to(h[tm - 1:tm, :], h_ref.shape)
    o_ref[...] = (h * _gelu(gate_ref[...])).astype(o_ref.dtype)


def _lru(proj, conv_w, conv_b, r_w, r_b, i_w, i_b, lru_l, *, batch, seq, tm):
    t = proj.shape[0]
    w = LRU_BLOCKS * LRU_BLOCK
    nblk = seq // tm
    nh = tm // SUBLANES
    vec = pl.BlockSpec((1, LRU_BLOCK), lambda bi, g, j: (0, g))
    mat = pl.BlockSpec((1, LRU_BLOCK, LRU_BLOCK), lambda bi, g, j: (g, 0, 0))
    row = lambda a: a.reshape(1, w)
    return pl.pallas_call(
        _lru_kernel, grid=(batch, LRU_BLOCKS, nblk),
        in_specs=[pl.BlockSpec((tm, LRU_BLOCK), lambda bi, g, j: (bi * nblk + j, LRU_BLOCKS + g)),
                  pl.BlockSpec((SUBLANES, LRU_BLOCK),
                               lambda bi, g, j: (jnp.maximum((bi * nblk + j) * nh - 1, 0), LRU_BLOCKS + g)),
                  pl.BlockSpec((tm, LRU_BLOCK), lambda bi, g, j: (bi * nblk + j, g)),
                  pl.BlockSpec((4, LRU_BLOCK), lambda bi, g, j: (0, g)), vec, mat, vec, mat, vec, vec],
        out_specs=pl.BlockSpec((tm, LRU_BLOCK), lambda bi, g, j: (bi * nblk + j, g)),
        out_shape=jax.ShapeDtypeStruct((t, w), BF16),
        scratch_shapes=[pltpu.VMEM((SUBLANES, LRU_BLOCK), F32)],
        compiler_params=_cparams(3), name="rglru")(
            proj, proj, proj, conv_w, row(conv_b), r_w.astype(BF16), row(r_b), i_w.astype(BF16), row(i_b),
            row(lru_l))


def _oddeven_merge_sort_network(n):
    pairs = []

    def merge(lo, hi, r):
        step = 2 * r
        if step < hi - lo:
            merge(lo, hi, step)
            merge(lo + r, hi, step)
            pairs.extend((i, i + r) for i in range(lo + r, hi - r, step))
        else:
            pairs.append((lo, lo + r))

    def sort(lo, hi):
        if hi - lo >= 1:
            mid = lo + (hi - lo) // 2
            sort(lo, mid)
            sort(mid + 1, hi)
            merge(lo, hi, 1)

    sort(0, n - 1)
    return tuple(pairs)


_SORT16 = _oddeven_merge_sort_network(PEER_KEYS // SUBLANES)


def _peer_topk_kernel(q_ref, sk_ref, id_o, gate_o, v_ref, i_ref, s_ref):
    tm = q_ref.shape[0]
    neg = -jnp.inf
    ngrp = PEER_KEYS // SUBLANES
    sub = _iota2((SUBLANES, LANES), 0).astype(F32)
    nt = lambda a, b: lax.dot_general(a, b, (((1,), (1,)), ((), ())), preferred_element_type=F32)

    scores = []
    for half in range(2):
        ah, al = _split(sk_ref[0, half])
        bh, bl = _split(q_ref[:, PEER_KEYS * half:PEER_KEYS * (half + 1)])
        scores.append(nt(ah, bh) + nt(al, bh) + nt(ah, bl))

    for lb in range(tm // LANES):
        lanes = slice(LANES * lb, LANES * (lb + 1))
        for half in range(2):
            rows = [scores[half][SUBLANES * a:SUBLANES * (a + 1), lanes] for a in range(ngrp)]
            keys = [sub + float(SUBLANES * a) for a in range(ngrp)]
            for i, j in _SORT16:
                swap = rows[j] > rows[i]
                rows[i], rows[j] = jnp.maximum(rows[i], rows[j]), jnp.minimum(rows[i], rows[j])
                keys[i], keys[j] = jnp.where(swap, keys[j], keys[i]), jnp.where(swap, keys[i], keys[j])
            for kk in range(PEER_TOPK):
                m = jnp.max(rows[0], axis=0, keepdims=True)
                idx = jnp.min(jnp.where(rows[0] == m, keys[0], 1e9), axis=0, keepdims=True)
                v_ref[half, kk:kk + 1, lanes] = m
                i_ref[half, kk:kk + 1, lanes] = idx
                win = keys[0] == idx
                for a in range(PEER_TOPK - 1 - kk):
                    rows[a] = jnp.where(win, rows[a + 1], rows[a])
                    keys[a] = jnp.where(win, keys[a + 1], keys[a])

        v1, v2 = v_ref[0, :, lanes], v_ref[1, :, lanes]
        i1, i2 = i_ref[0, :, lanes], i_ref[1, :, lanes]
        lo = [v1[0:SUBLANES] + v2[b:b + 1] for b in range(PEER_TOPK)]
        lo_id = [i1[0:SUBLANES] * PEER_KEYS + i2[b:b + 1] for b in range(PEER_TOPK)]
        hi = v1[SUBLANES:PEER_TOPK] + v2[0:1]
        hi_id = i1[SUBLANES:PEER_TOPK] * PEER_KEYS + i2[0:1]
        col_lo, col_hi = sub, sub + float(SUBLANES)
        for kk in range(PEER_TOPK):
            m = jnp.maximum(jnp.max(lo[0], axis=0, keepdims=True), jnp.max(hi, axis=0, keepdims=True))
            col = jnp.minimum(jnp.min(jnp.where(lo[0] == m, col_lo, 1e9), axis=0, keepdims=True),
                              jnp.min(jnp.where(hi == m, col_hi, 1e9), axis=0, keepdims=True))
            win_lo, win_hi = col_lo == col, col_hi == col
            s_ref[kk:kk + 1, lanes] = m
            id_o[0, kk:kk + 1, lanes] = jnp.maximum(
                jnp.max(jnp.where(win_lo, lo_id[0], -1.0), axis=0, keepdims=True),
                jnp.max(jnp.where(win_hi, hi_id, -1.0), axis=0, keepdims=True))
            for b in range(PEER_TOPK - 1 - kk):
                lo[b] = jnp.where(win_lo, lo[b + 1], lo[b])
                lo_id[b] = jnp.where(win_lo, lo_id[b + 1], lo_id[b])
            hi = jnp.where(win_hi, neg, hi)
    top = s_ref[...]
    e = jnp.exp(top - top[0:1])
    gate_o[0] = e / jnp.sum(e, axis=0, keepdims=True)


def _peer_topk(q, subkeys, *, tm):
    t = q.shape[0]
    out = pl.BlockSpec((1, PEER_TOPK, tm), lambda i, h: (h, 0, i))
    return pl.pallas_call(
        _peer_topk_kernel, grid=(t // tm, PEER_HEADS),
        in_specs=[pl.BlockSpec((tm, 2 * PEER_KEYS), lambda i, h: (i, h)),
                  pl.BlockSpec((1, 2, PEER_KEYS, PEER_KEYS), lambda i, h: (h, 0, 0, 0))],
        out_specs=[out, out],
        out_shape=[jax.ShapeDtypeStruct((PEER_HEADS, PEER_TOPK, t), F32)] * 2,
        scratch_shapes=[pltpu.VMEM((2, PEER_TOPK, tm), F32), pltpu.VMEM((2, PEER_TOPK, tm), F32),
                        pltpu.VMEM((PEER_TOPK, tm), F32)],
        compiler_params=_cparams(2), name="peer_topk")(q, subkeys)


GATE_TILE = 64
GATE_STRIDE = GATE_TILE + SUBLANES


def _peer_gates_kernel(i1_ref, i2_ref, g_ref, o_ref, s_ref):
    ciota = _iota2((PEER_KEYS, LANES), 0).astype(F32)

    def body(t, carry):
        i1 = i1_ref[pl.ds(t, 1), :]
        i2 = i2_ref[pl.ds(t, 1), :]
        g = g_ref[pl.ds(t, 1), :]
        a = jnp.where(ciota == i1, g, 0.0)
        bt = jnp.where(ciota == i2, 1.0, 0.0)
        s_ref[pl.ds(t, PEER_KEYS, stride=GATE_STRIDE), :] = _dot_nt(a, bt)
        return carry

    lax.fori_loop(0, GATE_TILE, body, 0, unroll=16)
    for c in range(PEER_KEYS):
        o_ref[c] = s_ref[GATE_STRIDE * c:GATE_STRIDE * c + GATE_TILE, :].astype(o_ref.dtype)


def _peer_gates(i1, i2, gates):
    t = i1.shape[0]
    tok = pl.BlockSpec((GATE_TILE, LANES), lambda i: (i, 0))
    return pl.pallas_call(
        _peer_gates_kernel, grid=(t // GATE_TILE,),
        in_specs=[tok, tok, tok],
        out_specs=pl.BlockSpec((PEER_KEYS, GATE_TILE, PEER_KEYS), lambda i: (0, i, 0)),
        out_shape=jax.ShapeDtypeStruct((PEER_KEYS, t, PEER_KEYS), BF16),
        scratch_shapes=[pltpu.VMEM((PEER_KEYS * GATE_STRIDE, PEER_KEYS), F32)],
        compiler_params=_cparams(1), name="peer_gates")(i1, i2, gates)


PEER_STEP = 8 * PEER_KEYS

def _peer_ffn_kernel(sc_ref, x_ref, xr_ref, ut_ref, v_ref, g_ref, o_ref):
    @pl.when(pl.program_id(1) == 0)
    def _():
        o_ref[...] = jnp.zeros_like(o_ref)

    nslab = PEER_STEP // PEER_KEYS
    xr = xr_ref[...] * sc_ref[0]
    h = jnp.dot(x_ref[...], ut_ref[...], preferred_element_type=F32)
    wgt = jnp.concatenate(
        [_gelu(h[:, PEER_KEYS * c:PEER_KEYS * (c + 1)] * xr) * g_ref[c].astype(F32) for c in range(nslab)], axis=1)
    amax = jnp.maximum(jnp.max(jnp.abs(wgt), axis=-1, keepdims=True), F8_TINY)
    w8 = (wgt * (F8_RANGE / amax)).astype(F8)
    part = jnp.dot(w8, v_ref[...], preferred_element_type=F32)
    o_ref[...] += part * (amax * (sc_ref[1] / F8_RANGE))


def _peer_ffn(x8, xr, ut8, v8, descale, gmat, *, tm):
    t, d = x8.shape
    ne = ut8.shape[1]
    return pl.pallas_call(
        _peer_ffn_kernel, grid=(t // tm, ne // PEER_STEP),
        in_specs=[pl.BlockSpec(memory_space=pltpu.SMEM),
                  pl.BlockSpec((tm, d), lambda i, j: (i, 0)),
                  pl.BlockSpec((tm, LANES), lambda i, j: (i, 0)),
                  pl.BlockSpec((d, PEER_STEP), lambda i, j: (0, j)),
                  pl.BlockSpec((PEER_STEP, d), lambda i, j: (j, 0)),
                  pl.BlockSpec((PEER_STEP // PEER_KEYS, tm, PEER_KEYS), lambda i, j: (j, i, 0))],
        out_specs=pl.BlockSpec((tm, d), lambda i, j: (i, 0)),
        out_shape=jax.ShapeDtypeStruct((t, d), F32),
        compiler_params=_cparams(2), name="peer_ffn")(descale, x8, xr, ut8, v8, gmat)


def _ple_kernel(x_ref, f_ref, p_ref, g_ref, b_ref, wg_ref, bg_ref, wp_ref, o_ref):
    x2 = _layer_norm(DN_ALPHA * x_ref[...] + f_ref[...], g_ref[...], b_ref[...])
    gate = _sigmoid(_dot(x2, wg_ref[...]) + bg_ref[...])
    o_ref[...] = x2 + gate * _dot(p_ref[...], wp_ref[...])


def _ple(x, ffn, p, ln_g, ln_b, w_gate, b_gate, w_proj, *, tm):
    t, d = x.shape
    pd = p.shape[1]
    tok = pl.BlockSpec((tm, d), lambda i: (i, 0))
    vec = pl.BlockSpec((1, d), lambda i: (0, 0))
    return pl.pallas_call(
        _ple_kernel, grid=(t // tm,),
        in_specs=[tok, tok, pl.BlockSpec((tm, pd), lambda i: (i, 0)), vec, vec,
                  pl.BlockSpec((d, d), lambda i: (0, 0)), vec, pl.BlockSpec((pd, d), lambda i: (0, 0))],
        out_specs=tok, out_shape=jax.ShapeDtypeStruct((t, d), F32),
        compiler_params=_cparams(1), name="ple")(
            x, ffn, p, ln_g.reshape(1, d), ln_b.reshape(1, d), w_gate.astype(BF16), b_gate.reshape(1, d),
            w_proj.astype(BF16))


def _fp8_table(tab):
    amax = jnp.maximum(jnp.max(jnp.abs(tab)), F8_TINY)
    return (tab * (F8_RANGE / amax)).astype(F8), amax * (1.0 / F8_RANGE)


def _peer_block(xn_b, x8, xr, w_q, subkeys, u_tab, v_tab):
    t = xn_b.shape[0]
    q = _matmul(xn_b, w_q.astype(BF16), tm=1024, tn=512)
    ids, gates = _peer_topk(q, subkeys, tm=1024)
    to_tok = lambda a: a.reshape(PEER_HEADS * PEER_TOPK, t).T
    ids = to_tok(ids)
    i1 = jnp.floor(ids * (1.0 / PEER_KEYS))
    i2 = ids - PEER_KEYS * i1
    gmat = _peer_gates(i1, i2, to_tok(gates))
    ut8, du = _fp8_table(u_tab.T)
    v8, dv = _fp8_table(v_tab)
    return _peer_ffn(x8, xr, ut8, v8, jnp.stack([du, dv]), gmat, tm=1024)


def _even_in_weight(w_in):
    w = RWKV_WIDTH
    rk = 3 * w
    lora = lambda c0, n: jnp.pad(w_in[:, c0:c0 + n], ((0, 0), (0, LANES - n)))
    g0 = rk + 256
    cols = [w_in[:, g0:g0 + 3 * GDN_WIDTH],
            w_in[:, 0:rk],
            w_in[:, g0 + 3 * GDN_WIDTH:g0 + 4 * GDN_WIDTH],
            lora(rk, 96), lora(rk + 96, 96), lora(rk + 192, 64),
            lora(g0 + 4 * GDN_WIDTH, 2 * GDN_HEADS)]
    return jnp.concatenate(cols, axis=1).astype(BF16)


def kernel(x, p, ev_w_in, ev_shift_mu, ev_rwkv_w0, ev_rwkv_w_up, ev_rwkv_a0, ev_rwkv_a_up, ev_rwkv_g_up, ev_rwkv_k_k, ev_rwkv_k_a, ev_rwkv_r_k, ev_rwkv_ln_g, ev_rwkv_ln_b, ev_gdn_conv, ev_gdn_a_log, ev_gdn_dt_bias, ev_gdn_norm_g, ev_w_out, od_w_in, od_conv_w, od_conv_b, od_r_w, od_r_b, od_i_w, od_i_b, od_lru_L, od_w_out, ln_mix_g, ln_mix_b, peer_w_q, peer_subkeys, peer_u, peer_v, ln_ffn_g, ln_ffn_b, ple_w_proj, ple_w_gate, ple_b_gate):
    batch, seq, d = x.shape
    t = batch * seq
    xt = x.reshape(t, d)
    ev = dict(ev_shift_mu=ev_shift_mu, ev_rwkv_w0=ev_rwkv_w0, ev_rwkv_w_up=ev_rwkv_w_up, ev_rwkv_a0=ev_rwkv_a0,
              ev_rwkv_a_up=ev_rwkv_a_up, ev_rwkv_g_up=ev_rwkv_g_up, ev_rwkv_k_k=ev_rwkv_k_k,
              ev_rwkv_k_a=ev_rwkv_k_a, ev_rwkv_r_k=ev_rwkv_r_k, ev_rwkv_ln_g=ev_rwkv_ln_g,
              ev_rwkv_ln_b=ev_rwkv_ln_b, ev_gdn_conv=ev_gdn_conv, ev_gdn_a_log=ev_gdn_a_log,
              ev_gdn_dt_bias=ev_gdn_dt_bias, ev_gdn_norm_g=ev_gdn_norm_g)
    for layer in range(p.shape[0]):
        j = layer // 2
        if layer % 2 == 0:
            proj = _matmul(xt, _even_in_weight(ev_w_in[j]), tm=1024, tn=512)
            y_a, y_b = _even_mixers(proj, batch, seq, {k: v[j] for k, v in ev.items()})
            w_out = ev_w_out[j].astype(BF16)
            xn, xn_b, x8, xr = _proj_ln([y_a, y_b], [w_out[:RWKV_WIDTH], w_out[RWKV_WIDTH:]], xt,
                                        ln_mix_g[layer], ln_mix_b[layer], tm=256)
        else:
            proj = _matmul(xt, od_w_in[j].astype(BF16), tm=1024, tn=512)
            y = _lru(proj, od_conv_w[j], od_conv_b[j], od_r_w[j], od_r_b[j], od_i_w[j], od_i_b[j],
                     od_lru_L[j], batch=batch, seq=seq, tm=512)
            xn, xn_b, x8, xr = _proj_ln([y], [od_w_out[j].astype(BF16)], xt, ln_mix_g[layer], ln_mix_b[layer],
                                        tm=256)
        ffn = _peer_block(xn_b, x8, xr, peer_w_q[layer], peer_subkeys[layer], peer_u[layer], peer_v[layer])
        xt = _ple(xn, ffn, p[layer].reshape(t, -1), ln_ffn_g[layer], ln_ffn_b[layer], ple_w_gate[layer],
                  ple_b_gate[layer], ple_w_proj[layer], tm=256)
    return xt.reshape(batch, seq, d)
```

```python
import functools

import jax
import jax.numpy as jnp
from jax import lax
from jax.experimental import pallas as pl
from jax.experimental.pallas import tpu as pltpu

F32 = jnp.float32
BF16 = jnp.bfloat16
F8 = jnp.float8_e4m3fn
F8_RANGE = 224.0
F8_TINY = 1e-30

D_MODEL = 2048
DEPTH = 2
DN_ALPHA = (2.0 * DEPTH) ** 0.25
LN_EPS = 1e-5

RWKV_WIDTH = 1024
RWKV_HEAD = 64
RWKV_GN_EPS = 64e-5
RWKV_CHUNK = 64
GDN_WIDTH = 1024
GDN_HEAD = 128
GDN_HEADS = 8
GDN_CHUNK = 128
GDN_NORM_EPS = 1e-6
LRU_BLOCKS = 8
LRU_BLOCK = 256
LRU_C = 8.0
PEER_HEADS = 8
PEER_KEYS = 128
PEER_TOPK = 16

LANES = 128
SUBLANES = 8
VMEM_LIMIT = 56 * 1024 * 1024

EV_QKV0 = 0
EV_RKV0 = 3072
EV_Z0 = 6144
EV_LORA0 = 7168
EV_BA0 = 7552
EV_COLS = 7680


def _cparams(n_axes):
    return pltpu.CompilerParams(dimension_semantics=("arbitrary",) * n_axes,
                                vmem_limit_bytes=VMEM_LIMIT)


def _dot(a, b):
    a, b = a.astype(BF16), b.astype(BF16)
    if a.ndim == 3:
        return lax.dot_general(a, b, (((2,), (1,)), ((0,), (0,))), preferred_element_type=F32)
    return jnp.dot(a, b, preferred_element_type=F32)


def _dot_nt(a, b):
    a, b = a.astype(BF16), b.astype(BF16)
    if a.ndim == 3:
        return lax.dot_general(a, b, (((2,), (2,)), ((0,), (0,))), preferred_element_type=F32)
    return lax.dot_general(a, b, (((1,), (1,)), ((), ())), preferred_element_type=F32)


def _dot_tn(a, b):
    return _dot(jnp.swapaxes(a, -1, -2), b)


def _split(x):
    hi = x.astype(BF16)
    lo = (x - hi.astype(F32)).astype(BF16)
    return hi, lo


def _dot_lhs2(a, b):
    hi, lo = _split(a)
    return _dot(hi, b) + _dot(lo, b)


def _dot_rhs2(a, b):
    hi, lo = _split(b)
    return _dot(a, hi) + _dot(a, lo)


def _dot3(a, b):
    ah, al = _split(a)
    bh, bl = _split(b)
    return _dot(ah, bh) + _dot(al, bh) + _dot(ah, bl)


def _sigmoid(x):
    return 1.0 / (1.0 + jnp.exp(-x))


def _softplus(x):
    return jnp.maximum(x, 0.0) + jnp.log(1.0 + jnp.exp(-jnp.abs(x)))


def _gelu(x):
    return 0.5 * x * (1.0 + jnp.tanh(0.7978845608028654 * (x + 0.044715 * (x * x * x))))


def _iota2(shape, axis):
    return lax.broadcasted_iota(jnp.int32, shape, axis)


def _layer_norm(x, g, b):
    mu = jnp.mean(x, axis=-1, keepdims=True)
    xc = x - mu
    var = jnp.mean(xc * xc, axis=-1, keepdims=True)
    return xc * lax.rsqrt(var + LN_EPS) * g + b


def _shift_rows(x, halo, k):
    rolled = pltpu.roll(x, k, 0)
    hr = pltpu.roll(halo, k, 0)
    row = _iota2(halo.shape, 0)
    top = jnp.where(row < k, hr, rolled[0:SUBLANES])
    return jnp.concatenate([top, rolled[SUBLANES:]], axis=0)


def _tri_inv(x, n, top):
    row = _iota2((n, n), 0)
    col = _iota2((n, n), 1)
    eye = (row == col).astype(F32)
    x0 = jnp.where((row >> 3) == (col >> 3), x, 0.0)
    x2 = _dot(x0, x0)
    x4 = _dot(x2, x2)
    t = eye + x0
    t = t + _dot(t, x2)
    t = t + _dot(t, x4)
    s = 8
    while s < top:
        sh = s.bit_length() - 1
        off = ((row >> (sh + 1)) == (col >> (sh + 1))) & ((row >> sh) != (col >> sh))
        xo = jnp.where(off, x, 0.0)
        t = t + _dot(_dot(t, xo), t)
        s *= 2
    return t


def _mm_kernel(x_ref, w_ref, o_ref):
    o_ref[...] = jnp.dot(x_ref[...].astype(BF16), w_ref[...],
                         preferred_element_type=F32).astype(o_ref.dtype)


def _matmul(x, w, *, tm, tn, out_dtype=F32):
    m, k = x.shape
    n = w.shape[1]
    return pl.pallas_call(
        _mm_kernel, grid=(m // tm, n // tn),
        in_specs=[pl.BlockSpec((tm, k), lambda i, j: (i, 0)),
                  pl.BlockSpec((k, tn), lambda i, j: (0, j))],
        out_specs=pl.BlockSpec((tm, tn), lambda i, j: (i, j)),
        out_shape=jax.ShapeDtypeStruct((m, n), out_dtype),
        compiler_params=_cparams(2), name="matmul")(x, w)


def _proj_ln_kernel(*refs, n_in):
    ys = refs[:n_in]
    ws = refs[n_in:2 * n_in]
    x_ref, g_ref, b_ref, wq_ref, o_ref, q_ref, o8_ref, r8_ref = refs[2 * n_in:]
    mix = _dot(ys[0][...], ws[0][...])
    for y_ref, w_ref in zip(ys[1:], ws[1:]):
        mix = mix + _dot(y_ref[...], w_ref[...])
    xn = _layer_norm(DN_ALPHA * x_ref[...] + mix, g_ref[...], b_ref[...])
    o_ref[...] = xn
    q_ref[...] = _dot(xn, wq_ref[...])
    amax = jnp.maximum(jnp.max(jnp.abs(xn), axis=-1, keepdims=True), F8_TINY)
    o8_ref[...] = (xn * (F8_RANGE / amax)).astype(F8)
    r8_ref[...] = jnp.broadcast_to(amax * (1.0 / F8_RANGE), r8_ref.shape)


def _proj_ln(ys, ws, x, g, b, w_q, *, tm):
    t, d = x.shape
    n_in = len(ys)
    in_specs = ([pl.BlockSpec((tm, y.shape[1]), lambda i: (i, 0)) for y in ys]
                + [pl.BlockSpec(w.shape, lambda i: (0, 0)) for w in ws]
                + [pl.BlockSpec((tm, d), lambda i: (i, 0)),
                   pl.BlockSpec((1, d), lambda i: (0, 0)),
                   pl.BlockSpec((1, d), lambda i: (0, 0)),
                   pl.BlockSpec(w_q.shape, lambda i: (0, 0))])
    return pl.pallas_call(
        functools.partial(_proj_ln_kernel, n_in=n_in), grid=(t // tm,),
        in_specs=in_specs,
        out_specs=[pl.BlockSpec((tm, d), lambda i: (i, 0))] * 3 + [pl.BlockSpec((tm, LANES), lambda i: (i, 0))],
        out_shape=[jax.ShapeDtypeStruct((t, d), F32), jax.ShapeDtypeStruct((t, w_q.shape[1]), F32),
                   jax.ShapeDtypeStruct((t, d), F8), jax.ShapeDtypeStruct((t, LANES), F32)],
        compiler_params=_cparams(1), name="proj_ln")(*ys, *ws, x, g.reshape(1, d), b.reshape(1, d),
                                                      w_q.astype(BF16))


def _rwkv_pre_kernel(p_ref, ph_ref, l_ref, lh_ref, mu_ref, mul_ref, w0_ref, wup_ref, a0_ref, aup_ref,
                     gup_ref, kk_ref, ka_ref, rk_ref, bd_ref,
                     r_o, lw_o, k_o, v_o, a_o, b_o, bonus_o, g_o, *, tiles_per_seq):
    first = (pl.program_id(0) % tiles_per_seq) == 0
    w = RWKV_WIDTH

    def mixed(ref, href, m_ref, c0, c1):
        x = ref[:, c0:c1]
        hrow = jnp.where(first, 0.0, href[SUBLANES - 1:SUBLANES, c0:c1])
        prev = pltpu.roll(x, 1, 0)
        prev = jnp.where(_iota2(x.shape, 0) == 0, hrow, prev)
        return x + (prev - x) * m_ref[:, c0:c1]

    def segsum(x):
        bd = bd_ref[...]
        return jnp.concatenate(
            [_dot_lhs2(x[:, LANES * p:LANES * (p + 1)], bd) for p in range(w // LANES)], axis=1)

    r = mixed(p_ref, ph_ref, mu_ref, 0, w)
    k = mixed(p_ref, ph_ref, mu_ref, w, 2 * w)
    v = mixed(p_ref, ph_ref, mu_ref, 2 * w, 3 * w)
    wd = mixed(l_ref, lh_ref, mul_ref, 0, LANES)
    ad = mixed(l_ref, lh_ref, mul_ref, LANES, 2 * LANES)
    gd = mixed(l_ref, lh_ref, mul_ref, 2 * LANES, 3 * LANES)

    w_log = -_softplus(-(w0_ref[...] + _dot3(jnp.tanh(wd), wup_ref[...]))) - 0.5
    lw_o[...] = -jnp.exp(w_log)
    a = _sigmoid(a0_ref[...] + _dot3(ad, aup_ref[...]))
    g_o[...] = _dot3(_sigmoid(gd), gup_ref[...])
    kkr = k * kk_ref[...]
    kk = kkr * lax.rsqrt(segsum(kkr * kkr) + 1e-6)
    k2 = k * (1.0 + (a - 1.0) * ka_ref[...])
    bonus_o[...] = segsum(r * k2 * rk_ref[...]) * v
    r_o[...] = r
    k_o[...] = k2
    v_o[...] = v
    a_o[...] = -kk
    b_o[...] = kk * a


def _rwkv_pre(proj, seq, mu_rkv, mu_lora, w0, w_up, a0, a_up, g_up, k_k, k_a, r_k, bd64, *, tm):
    t = proj.shape[0]
    w = RWKV_WIDTH
    nh = tm // SUBLANES
    row = lambda a: a.reshape(1, -1)
    full = lambda a: pl.BlockSpec(a.shape, lambda i: (0, 0))
    args = [row(mu_rkv), row(mu_lora), row(w0), w_up, row(a0), a_up, g_up, row(k_k), row(k_a), row(r_k), bd64]
    in_specs = [pl.BlockSpec((tm, 3 * w), lambda i: (i, EV_RKV0 // (3 * w))),
                pl.BlockSpec((SUBLANES, 3 * w), lambda i: (jnp.maximum(i * nh - 1, 0), EV_RKV0 // (3 * w))),
                pl.BlockSpec((tm, 512), lambda i: (i, EV_LORA0 // 512)),
                pl.BlockSpec((SUBLANES, 512), lambda i: (jnp.maximum(i * nh - 1, 0), EV_LORA0 // 512))]
    in_specs += [full(a) for a in args]
    out_spec = pl.BlockSpec((tm, w), lambda i: (i, 0))
    return pl.pallas_call(
        functools.partial(_rwkv_pre_kernel, tiles_per_seq=seq // tm), grid=(t // tm,),
        in_specs=in_specs, out_specs=[out_spec] * 8,
        out_shape=[jax.ShapeDtypeStruct((t, w), F32)] * 8,
        compiler_params=_cparams(1), name="rwkv_pre")(proj, proj, proj, proj, *args)


def _rwkv_chunk_kernel(r_ref, lw_ref, k_ref, v_ref, a_ref, b_ref, bonus_ref, g_ref, lng_ref, lnb_ref,
                       bd_ref, o_ref, z_ref, *, n_chunks):
    L = RWKV_CHUNK

    @pl.when(pl.program_id(2) == 0)
    def _():
        z_ref[...] = jnp.zeros_like(z_ref)

    nc = n_chunks
    lane = _iota2((nc, L, LANES), 2)
    head0 = lane < RWKV_HEAD
    row = _iota2((2 * L, 2 * L), 0)
    col = _iota2((2 * L, 2 * L), 1)
    strict = row > col
    incl = row >= col
    eye = row == col
    tril = jnp.broadcast_to((_iota2((L, L), 0) >= _iota2((L, L), 1)).astype(BF16), (nc, L, L))
    bd = bd_ref[...]

    def stack2(x):
        return jnp.concatenate([jnp.where(head0, x, 0.0), jnp.where(head0, 0.0, x)], axis=1)

    r, lw, k, v, a, b = (ref[...].reshape(nc, L, LANES) for ref in (r_ref, lw_ref, k_ref, v_ref, a_ref, b_ref))
    cl = _dot_rhs2(tril, lw)
    cl_last = cl[:, L - 1:L, :]
    e_in = jnp.exp(cl)
    e_out = jnp.exp(-cl)
    e_end = jnp.exp(cl_last - cl)
    rt = stack2(r * e_in)
    at = stack2(a * jnp.exp(cl - lw))
    kt = stack2(k * e_out)
    bt = stack2(b * e_out)
    kc = stack2(k * e_end)
    bc = stack2(b * e_end)
    vs = stack2(v)
    aa = _dot_nt(jnp.concatenate([at, rt], axis=1), jnp.concatenate([bt, kt], axis=1))
    a_ab = jnp.where(strict, aa[:, 0:2 * L, 0:2 * L], 0.0)
    a_ak = jnp.where(strict, aa[:, 0:2 * L, 2 * L:4 * L], 0.0)
    a_rb = jnp.where(incl, aa[:, 2 * L:4 * L, 0:2 * L], 0.0)
    a_rk = jnp.where(incl, aa[:, 2 * L:4 * L, 2 * L:4 * L], 0.0)
    tm = _tri_inv(a_ab, 2 * L, L)
    hat = _dot(tm, jnp.concatenate([at, _dot(a_ak, vs)], axis=2))
    w2 = jnp.concatenate([hat, jnp.concatenate([jnp.zeros_like(vs), vs], axis=2)], axis=1)
    ry = _dot(jnp.concatenate([a_rb, a_rk], axis=2), w2)
    r_hat = rt + ry[:, :, 0:2 * L]
    y_v = ry[:, :, 2 * L:4 * L]
    mn = _dot_tn(jnp.concatenate([bc, kc], axis=1), w2)
    m = jnp.where(eye, jnp.exp(cl_last), 0.0) + mn[:, :, 0:2 * L]
    n = mn[:, :, 2 * L:4 * L]

    z = z_ref[...]
    ys = []
    for c in range(nc):
        yc2 = _dot(r_hat[c], z) + y_v[c]
        z = _dot(m[c], z) + n[c]
        ys.append(yc2[0:L] + yc2[L:2 * L])
    z_ref[...] = z
    y = jnp.concatenate(ys, axis=0)

    mean = _dot_lhs2(y, bd) * (1.0 / RWKV_HEAD)
    yc = y - mean
    var = _dot_lhs2(yc * yc, bd) * (1.0 / RWKV_HEAD)
    yn = yc * lax.rsqrt(var + RWKV_GN_EPS) * lng_ref[...] + lnb_ref[...]
    o_ref[...] = ((yn + bonus_ref[...]) * g_ref[...]).astype(o_ref.dtype)


def _rwkv_chunks(r, lw, k, v, a, b, bonus, g, ln_g, ln_b, bd64, *, batch, seq, tb):
    t, w = r.shape
    npairs = w // LANES
    nblk = seq // tb
    tok = pl.BlockSpec((tb, LANES), lambda bi, p, j: (bi * nblk + j, p))
    par = pl.BlockSpec((1, LANES), lambda bi, p, j: (0, p))
    return pl.pallas_call(
        functools.partial(_rwkv_chunk_kernel, n_chunks=tb // RWKV_CHUNK),
        grid=(batch, npairs, nblk),
        in_specs=[tok] * 8 + [par, par, pl.BlockSpec((LANES, LANES), lambda bi, p, j: (0, 0))],
        out_specs=tok,
        out_shape=jax.ShapeDtypeStruct((t, w), BF16),
        scratch_shapes=[pltpu.VMEM((LANES, LANES), F32)],
        compiler_params=_cparams(3), name="rwkv_chunks")(
            r, lw, k, v, a, b, bonus, g, ln_g.reshape(1, w), ln_b.reshape(1, w), bd64)


def _gdn_pre_kernel(p_ref, ph_ref, ba_ref, cw_ref, alog_ref, dtb_ref,
                    q_o, k_o, kb_o, vb_o, gb_o, *, tiles_per_seq):
    first = (pl.program_id(0) % tiles_per_seq) == 0
    w = GDN_WIDTH
    tm = p_ref.shape[0]
    ones = jnp.ones((LANES, LANES), BF16)

    def conv_silu(c0, c1):
        x = p_ref[:, c0:c1]
        halo = jnp.where(first, 0.0, ph_ref[:, c0:c1])
        out = x * cw_ref[3:4, c0:c1]
        for kshift in (1, 2, 3):
            out = out + _shift_rows(x, halo, kshift) * cw_ref[3 - kshift:4 - kshift, c0:c1]
        return out * _sigmoid(out)

    ba = ba_ref[...]
    beta = _sigmoid(ba)
    glog = -jnp.exp(alog_ref[...]) * _softplus(ba + dtb_ref[...])
    srow = _iota2((LANES, LANES), 0)
    for h in range(GDN_HEADS):
        sl = slice(GDN_HEAD * h, GDN_HEAD * (h + 1))
        beta_b = _dot_lhs2(beta, (srow == h).astype(BF16))
        gb_o[:, sl] = _dot_lhs2(glog, (srow == GDN_HEADS + h).astype(BF16))
        q = conv_silu(GDN_HEAD * h, GDN_HEAD * (h + 1))
        k = conv_silu(w + GDN_HEAD * h, w + GDN_HEAD * (h + 1))
        v = conv_silu(2 * w + GDN_HEAD * h, 2 * w + GDN_HEAD * (h + 1))
        qn = q * lax.rsqrt(_dot_lhs2(q * q, ones) + 1e-6) * (GDN_HEAD ** -0.5)
        kn = k * lax.rsqrt(_dot_lhs2(k * k, ones) + 1e-6)
        q_o[:, sl] = qn.astype(BF16)
        k_o[:, sl] = kn.astype(BF16)
        kb_o[:, sl] = (kn * beta_b).astype(BF16)
        vb_o[:, sl] = (v * beta_b).astype(BF16)
    del tm


def _gdn_pre(proj, seq, conv_w, a_log, dt_bias, *, tm):
    t = proj.shape[0]
    w = GDN_WIDTH
    nh = tm // SUBLANES
    pad = lambda a: jnp.zeros((1, LANES), F32).at[0, GDN_HEADS:2 * GDN_HEADS].set(a)
    full = lambda a: pl.BlockSpec(a.shape, lambda i: (0, 0))
    args = [conv_w, pad(a_log), pad(dt_bias)]
    out_spec = pl.BlockSpec((tm, w), lambda i: (i, 0))
    return pl.pallas_call(
        functools.partial(_gdn_pre_kernel, tiles_per_seq=seq // tm), grid=(t // tm,),
        in_specs=[pl.BlockSpec((tm, 3 * w), lambda i: (i, EV_QKV0 // (3 * w))),
                  pl.BlockSpec((SUBLANES, 3 * w), lambda i: (jnp.maximum(i * nh - 1, 0), EV_QKV0 // (3 * w))),
                  pl.BlockSpec((tm, LANES), lambda i: (i, EV_BA0 // LANES))] + [full(a) for a in args],
        out_specs=[out_spec] * 5,
        out_shape=[jax.ShapeDtypeStruct((t, w), BF16)] * 4 + [jax.ShapeDtypeStruct((t, w), F32)],
        compiler_params=_cparams(1), name="gdn_pre")(proj, proj, proj, *args)


def _gdn_chunk_kernel(q_ref, k_ref, kb_ref, vb_ref, gb_ref, z_ref, ng_ref, o_ref, s_ref, *, n_chunks):
    C = GDN_CHUNK

    @pl.when(pl.program_id(2) == 0)
    def _():
        s_ref[...] = jnp.zeros_like(s_ref)

    row = _iota2((C, C), 0)
    col = _iota2((C, C), 1)
    strict = row > col
    incl = row >= col
    eye = row == col
    nc = n_chunks
    tril = jnp.broadcast_to(incl.astype(BF16), (nc, C, C))

    q, k, kb, vb = (ref[...].astype(F32).reshape(nc, C, GDN_HEAD) for ref in (q_ref, k_ref, kb_ref, vb_ref))
    gc = _dot_rhs2(tril, gb_ref[...].reshape(nc, C, GDN_HEAD))
    gct = jnp.swapaxes(gc, 1, 2)
    dm = jnp.exp(jnp.where(incl, gc - gct, -1e30))
    egc = jnp.exp(gc)
    g_last = gc[:, C - 1:C, :]
    kk = _dot_nt(kb, k)
    qk = _dot_nt(q, k)
    m_mat = jnp.where(strict, kk * dm, 0.0)
    attn = qk * dm
    tinv = _tri_inv(-m_mat, C, C)
    uw = _dot(tinv, jnp.concatenate([vb, kb * egc], axis=2))
    au = _dot(attn, uw)
    o_v = au[:, :, 0:C]
    q_hat = q * egc - au[:, :, C:2 * C]
    kd = k * jnp.exp(g_last - gc)
    ku = _dot_tn(kd, uw)
    nn = ku[:, :, 0:C]
    mm = jnp.where(eye, jnp.exp(g_last), 0.0) - ku[:, :, C:2 * C]

    s = s_ref[...]
    os_ = []
    for c in range(nc):
        os_.append(_dot(q_hat[c], s) + o_v[c])
        s = _dot(mm[c], s) + nn[c]
    s_ref[...] = s
    o = jnp.concatenate(os_, axis=0)
    o = o * lax.rsqrt(jnp.mean(o * o, axis=-1, keepdims=True) + GDN_NORM_EPS) * ng_ref[...]
    zz = z_ref[...]
    o_ref[...] = (o * (zz * _sigmoid(zz))).astype(o_ref.dtype)


def _gdn_chunks(q, k, kb, vb, gb, proj, norm_g, *, batch, seq, tb):
    t, w = q.shape
    nblk = seq // tb
    tok = pl.BlockSpec((tb, GDN_HEAD), lambda bi, h, j: (bi * nblk + j, h))
    zspec = pl.BlockSpec((tb, GDN_HEAD), lambda bi, h, j: (bi * nblk + j, EV_Z0 // GDN_HEAD + h))
    return pl.pallas_call(
        functools.partial(_gdn_chunk_kernel, n_chunks=tb // GDN_CHUNK),
        grid=(batch, GDN_HEADS, nblk),
        in_specs=[tok] * 5 + [zspec, pl.BlockSpec((1, GDN_HEAD), lambda bi, h, j: (0, 0))],
        out_specs=tok,
        out_shape=jax.ShapeDtypeStruct((t, w), BF16),
        scratch_shapes=[pltpu.VMEM((GDN_HEAD, GDN_HEAD), F32)],
        compiler_params=_cparams(3), name="gdn_chunks")(q, k, kb, vb, gb, proj, norm_g.reshape(1, GDN_HEAD))


def _head_block_ones():
    i = jnp.arange(LANES)
    return ((i[:, None] // RWKV_HEAD) == (i[None, :] // RWKV_HEAD)).astype(BF16)


def _pad_rows(a, rows):
    return jnp.zeros((rows,) + a.shape[1:], a.dtype).at[:a.shape[0]].set(a)


def _even_mixers(proj, batch, seq, prm):
    bd64 = _head_block_ones()
    mu = prm['ev_shift_mu']
    w = RWKV_WIDTH
    mu_lora = jnp.zeros((512,), F32)
    mu_lora = mu_lora.at[0:96].set(mu[3 * w:3 * w + 96]).at[128:224].set(mu[3 * w + 96:3 * w + 192])
    mu_lora = mu_lora.at[256:320].set(mu[3 * w + 192:3 * w + 256])
    pre = _rwkv_pre(proj, seq, mu[:3 * w], mu_lora, prm['ev_rwkv_w0'], _pad_rows(prm['ev_rwkv_w_up'], LANES),
                    prm['ev_rwkv_a0'], _pad_rows(prm['ev_rwkv_a_up'], LANES),
                    _pad_rows(prm['ev_rwkv_g_up'], LANES), prm['ev_rwkv_k_k'], prm['ev_rwkv_k_a'],
                    prm['ev_rwkv_r_k'].reshape(-1), bd64, tm=256)
    y_a = _rwkv_chunks(*pre, prm['ev_rwkv_ln_g'], prm['ev_rwkv_ln_b'], bd64, batch=batch, seq=seq, tb=1024)
    gpre = _gdn_pre(proj, seq, prm['ev_gdn_conv'], prm['ev_gdn_a_log'], prm['ev_gdn_dt_bias'], tm=256)
    y_b = _gdn_chunks(*gpre, proj, prm['ev_gdn_norm_g'], batch=batch, seq=seq, tb=2048)
    return y_a, y_b


def _lru_kernel(x_ref, xh_ref, gate_ref, cw_ref, cb_ref, rw_ref, rb_ref, iw_ref, ib_ref, l_ref,
                o_ref, h_ref):
    j = pl.program_id(2)
    tm = x_ref.shape[0]

    @pl.when(j == 0)
    def _():
        h_ref[...] = jnp.zeros_like(h_ref)

    x = x_ref[...]
    halo = jnp.where(j == 0, 0.0, xh_ref[...])
    xc = x * cw_ref[3:4, :] + cb_ref[...]
    for kshift in (1, 2, 3):
        xc = xc + _shift_rows(x, halo, kshift) * cw_ref[3 - kshift:4 - kshift, :]
    r = _sigmoid(_dot(xc, rw_ref[0]) + rb_ref[...])
    i = _sigmoid(_dot(xc, iw_ref[0]) + ib_ref[...])
    log_a = -LRU_C * r * _softplus(-l_ref[...])
    a = jnp.exp(log_a)
    b = jnp.sqrt(1.0 - jnp.exp(2.0 * log_a)) * (i * xc)
    row = _iota2(a.shape, 0)
    s = 1
    while s < tm:
        keep = row >= s
        a_sh = jnp.where(keep, pltpu.roll(a, s, 0), 1.0)
        b_sh = jnp.where(keep, pltpu.roll(b, s, 0), 0.0)
        b = a * b_sh + b
        a = a * a_sh
        s *= 2
    h = b + a * h_ref[0:1, :]
    h_ref[...] = jnp.broadcast_to(h[tm - 1:tm, :], h_ref.shape)
    o_ref[...] = (h * _gelu(gate_ref[...])).astype(o_ref.dtype)


def _lru(proj, conv_w, conv_b, r_w, r_b, i_w, i_b, lru_l, *, batch, seq, tm):
    t = proj.shape[0]
    w = LRU_BLOCKS * LRU_BLOCK
    nblk = seq // tm
    nh = tm // SUBLANES
    vec = pl.BlockSpec((1, LRU_BLOCK), lambda bi, g, j: (0, g))
    mat = pl.BlockSpec((1, LRU_BLOCK, LRU_BLOCK), lambda bi, g, j: (g, 0, 0))
    row = lambda a: a.reshape(1, w)
    return pl.pallas_call(
        _lru_kernel, grid=(batch, LRU_BLOCKS, nblk),
        in_specs=[pl.BlockSpec((tm, LRU_BLOCK), lambda bi, g, j: (bi * nblk + j, LRU_BLOCKS + g)),
                  pl.BlockSpec((SUBLANES, LRU_BLOCK),
                               lambda bi, g, j: (jnp.maximum((bi * nblk + j) * nh - 1, 0), LRU_BLOCKS + g)),
                  pl.BlockSpec((tm, LRU_BLOCK), lambda bi, g, j: (bi * nblk + j, g)),
                  pl.BlockSpec((4, LRU_BLOCK), lambda bi, g, j: (0, g)), vec, mat, vec, mat, vec, vec],
        out_specs=pl.BlockSpec((tm, LRU_BLOCK), lambda bi, g, j: (bi * nblk + j, g)),
        out_shape=jax.ShapeDtypeStruct((t, w), BF16),
        scratch_shapes=[pltpu.VMEM((SUBLANES, LRU_BLOCK), F32)],
        compiler_params=_cparams(3), name="rglru")(
            proj, proj, proj, conv_w, row(conv_b), r_w.astype(BF16), row(r_b), i_w.astype(BF16), row(i_b),
            row(lru_l))


def _oddeven_merge_sort_network(n):
    pairs = []

    def merge(lo, hi, r):
        step = 2 * r
        if step < hi - lo:
            merge(lo, hi, step)
            merge(lo + r, hi, step)
            pairs.extend((i, i + r) for i in range(lo + r, hi - r, step))
        else:
            pairs.append((lo, lo + r))

    def sort(lo, hi):
        if hi - lo >= 1:
            mid = lo + (hi - lo) // 2
            sort(lo, mid)
            sort(mid + 1, hi)
            merge(lo, hi, 1)

    sort(0, n - 1)
    return tuple(pairs)


_SORT16 = _oddeven_merge_sort_network(PEER_KEYS // SUBLANES)


def _peer_topk_kernel(q_ref, sk_ref, id_o, gate_o, v_ref, i_ref, s_ref):
    tm = q_ref.shape[0]
    neg = -jnp.inf
    ngrp = PEER_KEYS // SUBLANES
    sub = _iota2((SUBLANES, LANES), 0).astype(F32)
    nt = lambda a, b: lax.dot_general(a, b, (((1,), (1,)), ((), ())), preferred_element_type=F32)

    scores = []
    for half in range(2):
        ah, al = _split(sk_ref[0, half])
        bh, bl = _split(q_ref[:, PEER_KEYS * half:PEER_KEYS * (half + 1)])
        scores.append(nt(ah, bh) + nt(al, bh) + nt(ah, bl))

    for lb in range(tm // LANES):
        lanes = slice(LANES * lb, LANES * (lb + 1))
        for half in range(2):
            rows = [scores[half][SUBLANES * a:SUBLANES * (a + 1), lanes] for a in range(ngrp)]
            keys = [sub + float(SUBLANES * a) for a in range(ngrp)]
            for i, j in _SORT16:
                swap = rows[j] > rows[i]
                rows[i], rows[j] = jnp.maximum(rows[i], rows[j]), jnp.minimum(rows[i], rows[j])
                keys[i], keys[j] = jnp.where(swap, keys[j], keys[i]), jnp.where(swap, keys[i], keys[j])
            for kk in range(PEER_TOPK):
                m = jnp.max(rows[0], axis=0, keepdims=True)
                idx = jnp.min(jnp.where(rows[0] == m, keys[0], 1e9), axis=0, keepdims=True)
                v_ref[half, kk:kk + 1, lanes] = m
                i_ref[half, kk:kk + 1, lanes] = idx
                win = keys[0] == idx
                for a in range(PEER_TOPK - 1 - kk):
                    rows[a] = jnp.where(win, rows[a + 1], rows[a])
                    keys[a] = jnp.where(win, keys[a + 1], keys[a])

        v1, v2 = v_ref[0, :, lanes], v_ref[1, :, lanes]
        i1, i2 = i_ref[0, :, lanes], i_ref[1, :, lanes]
        lo = [v1[0:SUBLANES] + v2[b:b + 1] for b in range(PEER_TOPK)]
        lo_id = [i1[0:SUBLANES] * PEER_KEYS + i2[b:b + 1] for b in range(PEER_TOPK)]
        hi = v1[SUBLANES:PEER_TOPK] + v2[0:1]
        hi_id = i1[SUBLANES:PEER_TOPK] * PEER_KEYS + i2[0:1]
        col_lo, col_hi = sub, sub + float(SUBLANES)
        for kk in range(PEER_TOPK):
            m = jnp.maximum(jnp.max(lo[0], axis=0, keepdims=True), jnp.max(hi, axis=0, keepdims=True))
            col = jnp.minimum(jnp.min(jnp.where(lo[0] == m, col_lo, 1e9), axis=0, keepdims=True),
                              jnp.min(jnp.where(hi == m, col_hi, 1e9), axis=0, keepdims=True))
            win_lo, win_hi = col_lo == col, col_hi == col
            s_ref[kk:kk + 1, lanes] = m
            id_o[0, kk:kk + 1, lanes] = jnp.maximum(
                jnp.max(jnp.where(win_lo, lo_id[0], -1.0), axis=0, keepdims=True),
                jnp.max(jnp.where(win_hi, hi_id, -1.0), axis=0, keepdims=True))
            for b in range(PEER_TOPK - 1 - kk):
                lo[b] = jnp.where(win_lo, lo[b + 1], lo[b])
                lo_id[b] = jnp.where(win_lo, lo_id[b + 1], lo_id[b])
            hi = jnp.where(win_hi, neg, hi)
    top = s_ref[...]
    e = jnp.exp(top - top[0:1])
    gate_o[0] = e / jnp.sum(e, axis=0, keepdims=True)


def _peer_topk(q, subkeys, *, tm):
    t = q.shape[0]
    out = pl.BlockSpec((1, PEER_TOPK, tm), lambda i, h: (h, 0, i))
    return pl.pallas_call(
        _peer_topk_kernel, grid=(t // tm, PEER_HEADS),
        in_specs=[pl.BlockSpec((tm, 2 * PEER_KEYS), lambda i, h: (i, h)),
                  pl.BlockSpec((1, 2, PEER_KEYS, PEER_KEYS), lambda i, h: (h, 0, 0, 0))],
        out_specs=[out, out],
        out_shape=[jax.ShapeDtypeStruct((PEER_HEADS, PEER_TOPK, t), F32)] * 2,
        scratch_shapes=[pltpu.VMEM((2, PEER_TOPK, tm), F32), pltpu.VMEM((2, PEER_TOPK, tm), F32),
                        pltpu.VMEM((PEER_TOPK, tm), F32)],
        compiler_params=_cparams(2), name="peer_topk")(q, subkeys)


GATE_TILE = 64
GATE_STRIDE = GATE_TILE + SUBLANES


def _peer_gates_kernel(i1_ref, i2_ref, g_ref, o_ref, s_ref):
    ciota = _iota2((PEER_KEYS, LANES), 0).astype(F32)

    def body(t, carry):
        i1 = i1_ref[pl.ds(t, 1), :]
        i2 = i2_ref[pl.ds(t, 1), :]
        g = g_ref[pl.ds(t, 1), :]
        a = jnp.where(ciota == i1, g, 0.0)
        bt = jnp.where(ciota == i2, 1.0, 0.0)
        s_ref[pl.ds(t, PEER_KEYS, stride=GATE_STRIDE), :] = _dot_nt(a, bt)
        return carry

    lax.fori_loop(0, GATE_TILE, body, 0, unroll=16)
    for c in range(PEER_KEYS):
        o_ref[c] = s_ref[GATE_STRIDE * c:GATE_STRIDE * c + GATE_TILE, :].astype(o_ref.dtype)


def _peer_gates(i1, i2, gates):
    t = i1.shape[0]
    tok = pl.BlockSpec((GATE_TILE, LANES), lambda i: (i, 0))
    return pl.pallas_call(
        _peer_gates_kernel, grid=(t // GATE_TILE,),
        in_specs=[tok, tok, tok],
        out_specs=pl.BlockSpec((PEER_KEYS, GATE_TILE, PEER_KEYS), lambda i: (0, i, 0)),
        out_shape=jax.ShapeDtypeStruct((PEER_KEYS, t, PEER_KEYS), BF16),
        scratch_shapes=[pltpu.VMEM((PEER_KEYS * GATE_STRIDE, PEER_KEYS), F32)],
        compiler_params=_cparams(1), name="peer_gates")(i1, i2, gates)


PEER_STEP = 8 * PEER_KEYS

def _peer_ffn_kernel(sc_ref, x_ref, xr_ref, ut_ref, v_ref, g_ref, o_ref):
    @pl.when(pl.program_id(1) == 0)
    def _():
        o_ref[...] = jnp.zeros_like(o_ref)

    nslab = PEER_STEP // PEER_KEYS
    xr = xr_ref[...] * sc_ref[0]
    h = jnp.dot(x_ref[...], ut_ref[...], preferred_element_type=F32)
    wgt = jnp.concatenate(
        [_gelu(h[:, PEER_KEYS * c:PEER_KEYS * (c + 1)] * xr) * g_ref[c].astype(F32) for c in range(nslab)], axis=1)
    amax = jnp.maximum(jnp.max(jnp.abs(wgt), axis=-1, keepdims=True), F8_TINY)
    w8 = (wgt * (F8_RANGE / amax)).astype(F8)
    part = jnp.dot(w8, v_ref[...], preferred_element_type=F32)
    o_ref[...] += part * (amax * (sc_ref[1] / F8_RANGE))


def _peer_ffn(x8, xr, ut8, v8, descale, gmat, *, tm):
    t, d = x8.shape
    ne = ut8.shape[1]
    return pl.pallas_call(
        _peer_ffn_kernel, grid=(t // tm, ne // PEER_STEP),
        in_specs=[pl.BlockSpec(memory_space=pltpu.SMEM),
                  pl.BlockSpec((tm, d), lambda i, j: (i, 0)),
                  pl.BlockSpec((tm, LANES), lambda i, j: (i, 0)),
                  pl.BlockSpec((d, PEER_STEP), lambda i, j: (0, j)),
                  pl.BlockSpec((PEER_STEP, d), lambda i, j: (j, 0)),
                  pl.BlockSpec((PEER_STEP // PEER_KEYS, tm, PEER_KEYS), lambda i, j: (j, i, 0))],
        out_specs=pl.BlockSpec((tm, d), lambda i, j: (i, 0)),
        out_shape=jax.ShapeDtypeStruct((t, d), F32),
        compiler_params=_cparams(2), name="peer_ffn")(descale, x8, xr, ut8, v8, gmat)


def _ple_kernel(x_ref, f_ref, p_ref, g_ref, b_ref, wg_ref, bg_ref, wp_ref, o_ref):
    x2 = _layer_norm(DN_ALPHA * x_ref[...] + f_ref[...], g_ref[...], b_ref[...])
    gate = _sigmoid(_dot(x2, wg_ref[...]) + bg_ref[...])
    o_ref[...] = x2 + gate * _dot(p_ref[...], wp_ref[...])


def _ple(x, ffn, p, ln_g, ln_b, w_gate, b_gate, w_proj, *, tm):
    t, d = x.shape
    pd = p.shape[1]
    tok = pl.BlockSpec((tm, d), lambda i: (i, 0))
    vec = pl.BlockSpec((1, d), lambda i: (0, 0))
    return pl.pallas_call(
        _ple_kernel, grid=(t // tm,),
        in_specs=[tok, tok, pl.BlockSpec((tm, pd), lambda i: (i, 0)), vec, vec,
                  pl.BlockSpec((d, d), lambda i: (0, 0)), vec, pl.BlockSpec((pd, d), lambda i: (0, 0))],
        out_specs=tok, out_shape=jax.ShapeDtypeStruct((t, d), F32),
        compiler_params=_cparams(1), name="ple")(
            x, ffn, p, ln_g.reshape(1, d), ln_b.reshape(1, d), w_gate.astype(BF16), b_gate.reshape(1, d),
            w_proj.astype(BF16))


def _fp8_table(tab):
    amax = jnp.maximum(jnp.max(jnp.abs(tab)), F8_TINY)
    return (tab * (F8_RANGE / amax)).astype(F8), amax * (1.0 / F8_RANGE)


def _peer_block(q, x8, xr, subkeys, u_tab, v_tab):
    t = q.shape[0]
    ids, gates = _peer_topk(q, subkeys, tm=1024)
    to_tok = lambda a: a.reshape(PEER_HEADS * PEER_TOPK, t).T
    ids = to_tok(ids)
    i1 = jnp.floor(ids * (1.0 / PEER_KEYS))
    i2 = ids - PEER_KEYS * i1
    gmat = _peer_gates(i1, i2, to_tok(gates))
    ut8, du = _fp8_table(u_tab.T)
    v8, dv = _fp8_table(v_tab)
    return _peer_ffn(x8, xr, ut8, v8, jnp.stack([du, dv]), gmat, tm=1024)


def _even_in_weight(w_in):
    w = RWKV_WIDTH
    rk = 3 * w
    lora = lambda c0, n: jnp.pad(w_in[:, c0:c0 + n], ((0, 0), (0, LANES - n)))
    g0 = rk + 256
    cols = [w_in[:, g0:g0 + 3 * GDN_WIDTH],
            w_in[:, 0:rk],
            w_in[:, g0 + 3 * GDN_WIDTH:g0 + 4 * GDN_WIDTH],
            lora(rk, 96), lora(rk + 96, 96), lora(rk + 192, 64),
            lora(g0 + 4 * GDN_WIDTH, 2 * GDN_HEADS)]
    return jnp.concatenate(cols, axis=1).astype(BF16)


def kernel(x, p, ev_w_in, ev_shift_mu, ev_rwkv_w0, ev_rwkv_w_up, ev_rwkv_a0, ev_rwkv_a_up, ev_rwkv_g_up, ev_rwkv_k_k, ev_rwkv_k_a, ev_rwkv_r_k, ev_rwkv_ln_g, ev_rwkv_ln_b, ev_gdn_conv, ev_gdn_a_log, ev_gdn_dt_bias, ev_gdn_norm_g, ev_w_out, od_w_in, od_conv_w, od_conv_b, od_r_w, od_r_b, od_i_w, od_i_b, od_lru_L, od_w_out, ln_mix_g, ln_mix_b, peer_w_q, peer_subkeys, peer_u, peer_v, ln_ffn_g, ln_ffn_b, ple_w_proj, ple_w_gate, ple_b_gate):
    batch, seq, d = x.shape
    t = batch * seq
    xt = x.reshape(t, d)
    ev = dict(ev_shift_mu=ev_shift_mu, ev_rwkv_w0=ev_rwkv_w0, ev_rwkv_w_up=ev_rwkv_w_up, ev_rwkv_a0=ev_rwkv_a0,
              ev_rwkv_a_up=ev_rwkv_a_up, ev_rwkv_g_up=ev_rwkv_g_up, ev_rwkv_k_k=ev_rwkv_k_k,
              ev_rwkv_k_a=ev_rwkv_k_a, ev_rwkv_r_k=ev_rwkv_r_k, ev_rwkv_ln_g=ev_rwkv_ln_g,
              ev_rwkv_ln_b=ev_rwkv_ln_b, ev_gdn_conv=ev_gdn_conv, ev_gdn_a_log=ev_gdn_a_log,
              ev_gdn_dt_bias=ev_gdn_dt_bias, ev_gdn_norm_g=ev_gdn_norm_g)
    for layer in range(p.shape[0]):
        j = layer // 2
        if layer % 2 == 0:
            proj = _matmul(xt, _even_in_weight(ev_w_in[j]), tm=1024, tn=512)
            y_a, y_b = _even_mixers(proj, batch, seq, {k: v[j] for k, v in ev.items()})
            w_out = ev_w_out[j].astype(BF16)
            xn, q, x8, xr = _proj_ln([y_a, y_b], [w_out[:RWKV_WIDTH], w_out[RWKV_WIDTH:]], xt,
                                     ln_mix_g[layer], ln_mix_b[layer], peer_w_q[layer], tm=256)
        else:
            proj = _matmul(xt, od_w_in[j].astype(BF16), tm=1024, tn=512)
            y = _lru(proj, od_conv_w[j], od_conv_b[j], od_r_w[j], od_r_b[j], od_i_w[j], od_i_b[j],
                     od_lru_L[j], batch=batch, seq=seq, tm=512)
            xn, q, x8, xr = _proj_ln([y], [od_w_out[j].astype(BF16)], xt, ln_mix_g[layer], ln_mix_b[layer],
                                     peer_w_q[layer], tm=256)
        ffn = _peer_block(q, x8, xr, peer_subkeys[layer], peer_u[layer], peer_v[layer])
        xt = _ple(xn, ffn, p[layer].reshape(t, -1), ln_ffn_g[layer], ln_ffn_b[layer], ple_w_gate[layer],
                  ple_b_gate[layer], ple_w_proj[layer], tm=256)
    return xt.reshape(batch, seq, d)
```
